```python
import math
import jax, jax.numpy as jnp
from jax import lax
import numpy as np

D_MODEL = 1024
BATCH = 4
SEQ = 4096
DEPTH = 4
DEC_BATCH = 128
DEC_SEQ = 1
PAST_LEN = 2048
PAGE_SIZE = 128

N_EVEN = (DEPTH + 1) // 2
N_ODD = DEPTH // 2
PLE_DIM = 256
EPS = 1e-5
ROPE_THETA = 500000.0
ALPHA = (2 * DEPTH) ** 0.25
BETA = (8 * DEPTH) ** -0.25

D_INNER_A = D_MODEL
P_A = 64
H_A = D_INNER_A // P_A
G_A = 2
N_A = 128
CONV_W = 4
CONV_DIM = D_INNER_A + 2 * G_A * N_A
SSD_CHUNK = 128

H_B = 8
HD_B = D_MODEL // H_B
D_INNER_B = H_B * HD_B
ROT_B = HD_B // 4
MOBA_BLOCK = 256
MOBA_TOPK = 3
MOBA_Q_CHUNK = 32

H_C = 8
DK_C = D_MODEL // (2 * H_C)
DV_C = 2 * DK_C
D_INNER_C = H_C * DV_C
ROT_C = DK_C // 4
ATTN_Q_BLOCK = 128

SIZES_EVEN = (D_INNER_A, CONV_DIM, H_A, D_INNER_B, D_INNER_B, D_INNER_B, D_INNER_B)
SIZES_ODD = (H_C * 2 * DK_C, H_C * 2 * DK_C, D_INNER_C, D_INNER_C)
IN_EVEN = sum(SIZES_EVEN)
IN_ODD = sum(SIZES_ODD)

kernel_name = 'hybrid_ssd_moba_diffattn_decode_step'


def _split(x, sizes):
    offs = [int(o) for o in np.cumsum(sizes)[:-1]]
    return jnp.split(x, offs, axis=-1)


def rms_norm(x, w):
    xf = x.astype(jnp.float32)
    xf = xf * lax.rsqrt(jnp.mean(xf * xf, axis=-1, keepdims=True) + EPS)
    return (xf * w.astype(jnp.float32)).astype(x.dtype)


def gated_rms_norm(y, z, w):
    b, length, d = y.shape
    yz = (y * jax.nn.silu(z)).astype(jnp.float32).reshape(b, length, G_A, d // G_A)
    yz = yz * lax.rsqrt(jnp.mean(yz * yz, axis=-1, keepdims=True) + EPS)
    return (yz.reshape(b, length, d) * w.astype(jnp.float32)).astype(y.dtype)


def layer_norm(x, g, b):
    xf = x.astype(jnp.float32)
    xc = xf - jnp.mean(xf, axis=-1, keepdims=True)
    var = jnp.mean(xc * xc, axis=-1, keepdims=True)
    return (xc * lax.rsqrt(var + EPS) * g.astype(jnp.float32) + b.astype(jnp.float32)).astype(x.dtype)


def rope_partial(x, pos, rot_dim):
    half = rot_dim // 2
    inv = ROPE_THETA ** (-jnp.arange(half, dtype=jnp.float32) * (2.0 / rot_dim))
    ang = pos.astype(jnp.float32)[:, None] * inv[None, :]
    shape = (1, x.shape[1]) + (1,) * (x.ndim - 3) + (half,)
    cos = jnp.cos(ang).reshape(shape)
    sin = jnp.sin(ang).reshape(shape)
    xf = x.astype(jnp.float32)
    x1 = xf[..., :half]
    x2 = xf[..., half:rot_dim]
    return jnp.concatenate([x1 * cos - x2 * sin, x2 * cos + x1 * sin, xf[..., rot_dim:]], axis=-1).astype(x.dtype)


def gather_past(cache, li, page_table, tail):
    rows = cache[li, page_table]
    return rows.reshape((page_table.shape[0], page_table.shape[1] * PAGE_SIZE) + tail)


def causal_conv(conv_in, w, b, length):
    out = conv_in[:, 0:length] * w[0] + b
    for tap in range(1, CONV_W):
        out = out + conv_in[:, tap:tap + length] * w[tap]
    return out


def ssd_scan(x, dt, a, bm, cm, h0):
    b, length, nh, p = x.shape
    q = math.gcd(length, SSD_CHUNK)
    nc = length // q
    rep = nh // bm.shape[2]
    n = bm.shape[3]
    xf = x.astype(jnp.float32).reshape(b, nc, q, nh, p)
    dtc = dt.reshape(b, nc, q, nh)
    bh = jnp.repeat(bm.astype(jnp.float32), rep, axis=2).reshape(b, nc, q, nh, n)
    ch = jnp.repeat(cm.astype(jnp.float32), rep, axis=2).reshape(b, nc, q, nh, n)
    acs = jnp.cumsum(dtc * a, axis=2)
    causal = jnp.tril(jnp.ones((q, q), dtype=bool))
    seg = acs[:, :, :, None, :] - acs[:, :, None, :, :]
    decay = jnp.exp(jnp.where(causal[None, None, :, :, None], seg, -jnp.inf))
    cb = jnp.einsum('bcihn,bcjhn->bcijh', ch, bh)
    y_diag = jnp.einsum('bcijh,bcjh,bcjhp->bcihp', cb * decay, dtc, xf)
    to_end = jnp.exp(acs[:, :, -1:, :] - acs)
    states = jnp.einsum('bcjhn,bcjh,bcjhp->bchpn', bh, to_end * dtc, xf)
    chunk_decay = jnp.exp(acs[:, :, -1, :])

    def step(h, inp):
        s, d = inp
        return h * d[:, :, None, None] + s, h

    h_last, h_in = lax.scan(step, h0.astype(jnp.float32),
                            (states.transpose(1, 0, 2, 3, 4), chunk_decay.transpose(1, 0, 2)))
    h_in = h_in.transpose(1, 0, 2, 3, 4)
    y_off = jnp.einsum('bcihn,bchpn,bcih->bcihp', ch, h_in, jnp.exp(acs))
    y = (y_diag + y_off).reshape(b, length, nh, p)
    return y.astype(x.dtype), h_last.astype(h0.dtype)


def moba_attention(q, k_all, v_all, pos0):
    b, lq, nh, hd = q.shape
    nb = k_all.shape[1] // MOBA_BLOCK
    kb = k_all.reshape(b, nb, MOBA_BLOCK, nh, hd)
    vb = v_all.reshape(b, nb, MOBA_BLOCK, nh, hd)
    kmean = jnp.mean(kb.astype(jnp.float32), axis=2)
    k_sel = min(MOBA_TOPK, nb)
    chunk = math.gcd(lq, MOBA_Q_CHUNK)
    nq = lq // chunk
    q_chunks = q.reshape(b, nq, chunk, nh, hd).transpose(1, 0, 2, 3, 4)
    pos_chunks = (pos0 + jnp.arange(lq, dtype=jnp.int32)).reshape(nq, chunk)
    bi = jnp.arange(b)[:, None, None, None]
    hi = jnp.arange(nh)[None, None, :, None]
    blk_ids = jnp.arange(nb, dtype=jnp.int32)
    scale = hd ** -0.5

    def attend(args):
        qq, pos = args
        own = pos // MOBA_BLOCK
        gate = jnp.einsum('bqhd,bnhd->bqhn', qq.astype(jnp.float32), kmean)
        fully_past = blk_ids[None, :] < own[:, None]
        gate = jnp.where(fully_past[None, :, None, :], gate, -jnp.inf)
        _, top = lax.top_k(gate, k_sel)
        idx = jnp.concatenate([top.astype(jnp.int32),
                               jnp.broadcast_to(own[None, :, None, None], (b, chunk, nh, 1))], axis=-1)
        slot_ok = jnp.concatenate([jnp.arange(k_sel)[None, :] < own[:, None],
                                   jnp.ones((chunk, 1), dtype=bool)], axis=1)
        kg = kb[bi, idx, :, hi]
        vg = vb[bi, idx, :, hi]
        kpos = idx[..., None] * MOBA_BLOCK + jnp.arange(MOBA_BLOCK, dtype=jnp.int32)
        mask = slot_ok[None, :, None, :, None] & (kpos <= pos[None, :, None, None, None])
        s = jnp.einsum('bqhd,bqhskd->bqhsk', qq, kg).astype(jnp.float32) * scale
        s = jnp.where(mask, s, -jnp.inf).reshape(b, chunk, nh, -1)
        p = jax.nn.softmax(s, axis=-1).reshape(mask.shape).astype(vg.dtype)
        return jnp.einsum('bqhsk,bqhskd->bqhd', p, vg)

    out = lax.map(attend, (q_chunks, pos_chunks))
    return out.transpose(1, 0, 2, 3, 4).reshape(b, lq, nh, hd)


def diff_attention(q, k, v, pos0, lam):
    b, lq, nh, _, dk = q.shape
    lk = k.shape[1]
    chunk = math.gcd(lq, ATTN_Q_BLOCK)
    nq = lq // chunk
    q_chunks = q.reshape(b, nq, chunk, nh, 2, dk).transpose(1, 0, 2, 3, 4, 5)
    pos_chunks = (pos0 + jnp.arange(lq, dtype=jnp.int32)).reshape(nq, chunk)
    kpos = jnp.arange(lk, dtype=jnp.int32)
    scale = dk ** -0.5

    def attend(args):
        qq, pos = args
        s = jnp.einsum('bqhmd,bkhmd->bhmqk', qq, k).astype(jnp.float32) * scale
        s = jnp.where((kpos[None, :] <= pos[:, None])[None, None, None], s, -jnp.inf)
        p = jax.nn.softmax(s, axis=-1)
        pd = (p[:, :, 0] - lam * p[:, :, 1]).astype(v.dtype)
        return jnp.einsum('bhqk,bkhd->bqhd', pd, v)

    out = lax.map(attend, (q_chunks, pos_chunks))
    return out.transpose(1, 0, 2, 3, 4).reshape(b, lq, nh, v.shape[-1])


def even_mixer(h, pos0, cache_k, cache_v, page_table, li, ssm0, conv0,
               w_in, conv_w, conv_b, dt_bias, a_log, d_skip, norm_w, w_out):
    b, length, _ = h.shape
    proj = jnp.einsum('bld,de->ble', h, w_in)
    z, xbc, dt_raw, q, k, v, g = _split(proj, SIZES_EVEN)
    conv_in = jnp.concatenate([conv0, xbc], axis=1)
    conv_new = conv_in[:, length:]
    xbc = jax.nn.silu(causal_conv(conv_in, conv_w, conv_b, length))
    xs, bm, cm = _split(xbc, (D_INNER_A, G_A * N_A, G_A * N_A))
    dt = jax.nn.softplus(dt_raw.astype(jnp.float32) + dt_bias.astype(jnp.float32))
    a = -jnp.exp(a_log.astype(jnp.float32))
    xh = xs.reshape(b, length, H_A, P_A)
    y, ssm_new = ssd_scan(xh, dt, a, bm.reshape(b, length, G_A, N_A), cm.reshape(b, length, G_A, N_A), ssm0)
    y = y + d_skip[:, None] * xh
    y_a = gated_rms_norm(y.reshape(b, length, D_INNER_A), z, norm_w)
    pos = pos0 + jnp.arange(length, dtype=jnp.int32)
    qh = rope_partial(q.reshape(b, length, H_B, HD_B), pos, ROT_B)
    kh = rope_partial(k.reshape(b, length, H_B, HD_B), pos, ROT_B)
    vh = v.reshape(b, length, H_B, HD_B)
    k_parts, v_parts = [kh], [vh]
    if cache_k is not None:
        k_parts = [gather_past(cache_k, li, page_table, (H_B, HD_B)), kh]
        v_parts = [gather_past(cache_v, li, page_table, (H_B, HD_B)), vh]
    lk = pos0 + length
    nb = -(-lk // MOBA_BLOCK)
    pad = jnp.zeros((b, nb * MOBA_BLOCK - lk, H_B, HD_B), kh.dtype)
    k_all = jnp.concatenate(k_parts + [pad], axis=1)
    v_all = jnp.concatenate(v_parts + [pad.astype(vh.dtype)], axis=1)
    o = moba_attention(qh, k_all, v_all, pos0)
    y_b = o.reshape(b, length, D_INNER_B) * jax.nn.silu(g)
    out = jnp.einsum('ble,ed->bld', jnp.concatenate([y_a, y_b], axis=-1), w_out)
    return out, kh, vh, ssm_new, conv_new


def odd_mixer(h, pos0, layer_idx, cache_k, cache_v, page_table, li,
              w_in, lq1, lk1, lq2, lk2, subln_w, w_out):
    b, length, _ = h.shape
    f32 = jnp.float32
    proj = jnp.einsum('bld,de->ble', h, w_in)
    q, k, v, g = _split(proj, SIZES_ODD)
    pos = pos0 + jnp.arange(length, dtype=jnp.int32)
    qh = rope_partial(q.reshape(b, length, H_C, 2, DK_C), pos, ROT_C)
    kh = rope_partial(k.reshape(b, length, H_C, 2, DK_C), pos, ROT_C)
    vh = v.reshape(b, length, H_C, DV_C)
    k_all, v_all = kh, vh
    if cache_k is not None:
        k_all = jnp.concatenate([gather_past(cache_k, li, page_table, (H_C, 2, DK_C)), kh], axis=1)
        v_all = jnp.concatenate([gather_past(cache_v, li, page_table, (H_C, DV_C)), vh], axis=1)
    lam_init = 0.8 - 0.6 * math.exp(-0.3 * layer_idx)
    lam = (jnp.exp(jnp.sum(lq1.astype(f32) * lk1.astype(f32)))
           - jnp.exp(jnp.sum(lq2.astype(f32) * lk2.astype(f32))) + lam_init)
    o = diff_attention(qh, k_all, v_all, pos0, lam)
    o = rms_norm(o, subln_w) * (1.0 - lam_init)
    y = o.reshape(b, length, D_INNER_C) * jax.nn.silu(g)
    out = jnp.einsum('ble,ed->bld', y, w_out)
    return out, kh.reshape(b, length, H_C, 2 * DK_C), vh


def post_block(x, out, p_i, g, b, w_gate, w_proj):
    x = layer_norm(ALPHA * x + out, g, b)
    return x + jax.nn.sigmoid(jnp.einsum('bld,de->ble', x, w_gate)) * jnp.einsum('blp,pd->bld', p_i, w_proj)


def setup_inputs(seed: int = 0) -> dict:
    key = jax.random.key(seed)
    ks = jax.random.split(key, 32)
    f32 = jnp.float32

    def nrm(k, shape, scale):
        return jax.random.normal(k, shape, f32) * scale

    n_pages = PAST_LEN // PAGE_SIZE
    n_used = DEC_BATCH * n_pages
    n_pool = n_used + n_used // 4
    perm = jax.random.permutation(ks[0], n_pool)
    page_table = perm[:n_used].reshape(DEC_BATCH, n_pages).astype(jnp.int32)
    dt0 = jnp.exp(jax.random.uniform(ks[1], (N_EVEN, H_A), f32, math.log(1e-3), math.log(1e-1)))
    return {
        'x_prompt': nrm(ks[2], (BATCH, SEQ, D_MODEL), 1.0),
        'x_sample': nrm(ks[3], (DEC_BATCH, DEC_SEQ, D_MODEL), 1.0),
        'cache_moba_k': nrm(ks[4], (N_EVEN, n_pool, PAGE_SIZE, H_B, HD_B), 1.0),
        'cache_moba_v': nrm(ks[5], (N_EVEN, n_pool, PAGE_SIZE, H_B, HD_B), 1.0),
        'cache_diff_k': nrm(ks[6], (N_ODD, n_pool, PAGE_SIZE, H_C, 2 * DK_C), 1.0),
        'cache_diff_v': nrm(ks[7], (N_ODD, n_pool, PAGE_SIZE, H_C, DV_C), 1.0),
        'state_ssm': nrm(ks[8], (N_EVEN, DEC_BATCH, H_A, P_A, N_A), 0.1),
        'state_conv': nrm(ks[9], (N_EVEN, DEC_BATCH, CONV_W - 1, CONV_DIM), 1.0),
        'page_table': page_table,
        'p_prompt': nrm(ks[10], (DEPTH, BATCH, SEQ, PLE_DIM), 1.0),
        'p_sample': nrm(ks[11], (DEPTH, DEC_BATCH, DEC_SEQ, PLE_DIM), 1.0),
        'w_in_even': nrm(ks[12], (N_EVEN, D_MODEL, IN_EVEN), D_MODEL ** -0.5),
        'conv_w': nrm(ks[13], (N_EVEN, CONV_W, CONV_DIM), CONV_W ** -0.5),
        'conv_b': nrm(ks[14], (N_EVEN, CONV_DIM), 0.02),
        'dt_bias': dt0 + jnp.log(-jnp.expm1(-dt0)),
        'a_log': jnp.log(jax.random.uniform(ks[15], (N_EVEN, H_A), f32, 1.0, 16.0)),
        'd_skip': 1.0 + nrm(ks[16], (N_EVEN, H_A), 0.1),
        'ssm_norm_w': 1.0 + nrm(ks[17], (N_EVEN, D_INNER_A), 0.02),
        'w_out_even': nrm(ks[18], (N_EVEN, D_INNER_A + D_INNER_B, D_MODEL), (D_INNER_A + D_INNER_B) ** -0.5 * BETA),
        'w_in_odd': nrm(ks[19], (N_ODD, D_MODEL, IN_ODD), D_MODEL ** -0.5),
        'lambda_q1': nrm(ks[20], (N_ODD, DK_C), 0.1),
        'lambda_k1': nrm(ks[21], (N_ODD, DK_C), 0.1),
        'lambda_q2': nrm(ks[22], (N_ODD, DK_C), 0.1),
        'lambda_k2': nrm(ks[23], (N_ODD, DK_C), 0.1),
        'subln_w': 1.0 + nrm(ks[24], (N_ODD, DV_C), 0.02),
        'w_out_odd': nrm(ks[25], (N_ODD, D_INNER_C, D_MODEL), D_INNER_C ** -0.5 * BETA),
        'ln_g': 1.0 + nrm(ks[26], (DEPTH, D_MODEL), 0.02),
        'ln_b': nrm(ks[27], (DEPTH, D_MODEL), 0.02),
        'w_ple_gate': nrm(ks[28], (DEPTH, D_MODEL, D_MODEL), D_MODEL ** -0.5),
        'w_ple_proj': nrm(ks[29], (DEPTH, PLE_DIM, D_MODEL), PLE_DIM ** -0.5),
    }


def reference(x_prompt, x_sample, cache_moba_k, cache_moba_v, cache_diff_k, cache_diff_v,
              state_ssm, state_conv, page_table, p_prompt, p_sample,
              w_in_even, conv_w, conv_b, dt_bias, a_log, d_skip, ssm_norm_w, w_out_even,
              w_in_odd, lambda_q1, lambda_k1, lambda_q2, lambda_k2, subln_w, w_out_odd,
              ln_g, ln_b, w_ple_gate, w_ple_proj):
    xp, xs = x_prompt, x_sample
    bp = xp.shape[0]
    past_len = page_table.shape[1] * PAGE_SIZE
    pm_k, pm_v, pd_k, pd_v, p_ssm, p_conv = [], [], [], [], [], []
    sm_k, sm_v, sd_k, sd_v, s_ssm, s_conv = [], [], [], [], [], []
    for i in range(DEPTH):
        li = i // 2
        if i % 2 == 0:
            ew = (w_in_even[li], conv_w[li], conv_b[li], dt_bias[li], a_log[li], d_skip[li],
                  ssm_norm_w[li], w_out_even[li])
            ssm0 = jnp.zeros((bp, H_A, P_A, N_A), xp.dtype)
            conv0 = jnp.zeros((bp, CONV_W - 1, CONV_DIM), xp.dtype)
            out_p, k_p, v_p, st_p, cv_p = even_mixer(xp, 0, None, None, None, li, ssm0, conv0, *ew)
            out_s, k_s, v_s, st_s, cv_s = even_mixer(xs, past_len, cache_moba_k, cache_moba_v, page_table, li,
                                                     state_ssm[li], state_conv[li], *ew)
            pm_k.append(k_p); pm_v.append(v_p); p_ssm.append(st_p); p_conv.append(cv_p)
            sm_k.append(k_s); sm_v.append(v_s); s_ssm.append(st_s); s_conv.append(cv_s)
        else:
            ow = (w_in_odd[li], lambda_q1[li], lambda_k1[li], lambda_q2[li], lambda_k2[li],
                  subln_w[li], w_out_odd[li])
            out_p, k_p, v_p = odd_mixer(xp, 0, i, None, None, None, li, *ow)
            out_s, k_s, v_s = odd_mixer(xs, past_len, i, cache_diff_k, cache_diff_v, page_table, li, *ow)
            pd_k.append(k_p); pd_v.append(v_p)
            sd_k.append(k_s); sd_v.append(v_s)
        xp = post_block(xp, out_p, p_prompt[i], ln_g[i], ln_b[i], w_ple_gate[i], w_ple_proj[i])
        xs = post_block(xs, out_s, p_sample[i], ln_g[i], ln_b[i], w_ple_gate[i], w_ple_proj[i])
    y_prompt = xp
    y_sample = xs
    prompt_moba_k = jnp.stack(pm_k)
    prompt_moba_v = jnp.stack(pm_v)
    prompt_diff_k = jnp.stack(pd_k)
    prompt_diff_v = jnp.stack(pd_v)
    prompt_ssm = jnp.stack(p_ssm)
    prompt_conv = jnp.stack(p_conv)
    sample_moba_k = jnp.stack(sm_k)
    sample_moba_v = jnp.stack(sm_v)
    sample_diff_k = jnp.stack(sd_k)
    sample_diff_v = jnp.stack(sd_v)
    sample_ssm = jnp.stack(s_ssm)
    sample_conv = jnp.stack(s_conv)
    return (y_prompt, y_sample, prompt_moba_k, prompt_moba_v, prompt_diff_k, prompt_diff_v,
            prompt_ssm, prompt_conv, sample_moba_k, sample_moba_v, sample_diff_k, sample_diff_v,
            sample_ssm, sample_conv)
```

```python
import functools
import math

import jax
import jax.numpy as jnp
from jax import lax
from jax.experimental import pallas as pl
from jax.experimental.pallas import tpu as pltpu

F32 = jnp.float32
BF16 = jnp.bfloat16

D_MODEL = 1024
DEPTH = 4
PAGE_SIZE = 128
PLE_DIM = 256
EPS = 1e-5
ROPE_THETA = 500000.0
ALPHA = (2 * DEPTH) ** 0.25

D_INNER_A = D_MODEL
P_A = 64
H_A = D_INNER_A // P_A
G_A = 2
N_A = 128
CONV_W = 4
CONV_DIM = D_INNER_A + 2 * G_A * N_A
SSD_CHUNK = 128

H_B = 8
HD_B = D_MODEL // H_B
D_INNER_B = H_B * HD_B
ROT_B = HD_B // 4
MOBA_BLOCK = 256
MOBA_TOPK = 3

H_C = 8
DK_C = D_MODEL // (2 * H_C)
DV_C = 2 * DK_C
D_INNER_C = H_C * DV_C
ROT_C = DK_C // 4

LANES = 128
ATTN_TILE = 256
VMEM_LIMIT = 56 * 1024 * 1024
NEG_INF = float("-inf")


def _split2(a):
    hi = a.astype(BF16)
    lo = (a - hi.astype(F32)).astype(BF16)
    return hi, lo


def _split3(a):
    hi = a.astype(BF16)
    r = a - hi.astype(F32)
    mid = r.astype(BF16)
    lo = (r - mid.astype(F32)).astype(BF16)
    return hi, mid, lo


def _mm(a, b):
    return jnp.dot(a.astype(BF16), b.astype(BF16), preferred_element_type=F32)


def _mm_nt(a, b):
    return lax.dot_general(a.astype(BF16), b.astype(BF16), (((1,), (1,)), ((), ())),
                           preferred_element_type=F32)


def _mm_sel_rhs(a, e, parts=3):
    eb = e.astype(BF16)
    pieces = _split3(a) if parts == 3 else _split2(a)
    out = jnp.dot(pieces[0], eb, preferred_element_type=F32)
    for p in pieces[1:]:
        out = out + jnp.dot(p, eb, preferred_element_type=F32)
    return out


def _mm_sel_lhs(e, b):
    eb = e.astype(BF16)
    pieces = _split3(b)
    out = jnp.dot(eb, pieces[0], preferred_element_type=F32)
    for p in pieces[1:]:
        out = out + jnp.dot(eb, p, preferred_element_type=F32)
    return out


def _mm_nt_hp(a, b):
    ah, al = _split2(a)
    bh, bl = _split2(b)
    dn = (((1,), (1,)), ((), ()))
    return (lax.dot_general(ah, bh, dn, preferred_element_type=F32)
            + lax.dot_general(ah, bl, dn, preferred_element_type=F32)
            + lax.dot_general(al, bh, dn, preferred_element_type=F32))


def _rows8(row):
    return jnp.broadcast_to(row, (8, row.shape[1]))


def _row_sel(row, e):
    return _mm_sel_rhs(_rows8(row), e)[0:1, :]


def _silu(x):
    return x * jax.nn.sigmoid(x)


def _softplus(x):
    return jnp.maximum(x, 0.0) + jnp.log1p(jnp.exp(-jnp.abs(x)))


def _rope128(x, c, s1, s2, r):
    return x * c + pltpu.roll(x, LANES - r, 1) * s1 + pltpu.roll(x, r, 1) * s2


def _seg_matrix(n_rows, n_cols, seg, transpose=False):
    if not transpose:
        lane = lax.broadcasted_iota(jnp.int32, (n_rows, n_cols), 0)
        sid = lax.broadcasted_iota(jnp.int32, (n_rows, n_cols), 1)
    else:
        sid = lax.broadcasted_iota(jnp.int32, (n_rows, n_cols), 0)
        lane = lax.broadcasted_iota(jnp.int32, (n_rows, n_cols), 1)
    return jnp.where(lane // seg == sid, 1.0, 0.0).astype(F32)


def _const_spec(shape):
    nd = len(shape)
    return pl.BlockSpec(shape, lambda *_: (0,) * nd, pipeline_mode=pl.Buffered(1))


def _params(n_grid):
    return pltpu.CompilerParams(dimension_semantics=("arbitrary",) * n_grid,
                                vmem_limit_bytes=VMEM_LIMIT)


_PROJ_CHUNK = 512


def _proj_into(xb, w_ref, o_ref, width, rope=None):
    for c0 in range(0, width, _PROJ_CHUNK):
        cw = min(_PROJ_CHUNK, width - c0)
        r = jnp.dot(xb, w_ref[:, c0:c0 + cw], preferred_element_type=F32)
        if rope is None:
            o_ref[:, c0:c0 + cw] = r
        else:
            c, s1, s2, shift = rope
            for j in range(cw // LANES):
                o_ref[:, c0 + j * LANES:c0 + (j + 1) * LANES] = _rope128(
                    r[:, j * LANES:(j + 1) * LANES], c, s1, s2, shift)


def _inproj_even_kernel(x_ref, c_ref, s1_ref, s2_ref, wz, wxbc, wdt, wdtT, wq, wk, wv, wg,
                        dtb, dtbT, z_o, xbc_o, dt_o, dtT_o, q_o, k_o, v_o, g_o):
    xb = x_ref[...].astype(BF16)
    rope = (c_ref[...], s1_ref[...], s2_ref[...], ROT_B // 2)
    _proj_into(xb, wz, z_o, D_INNER_A)
    _proj_into(xb, wxbc, xbc_o, CONV_DIM)
    dt_o[...] = _softplus(jnp.dot(xb, wdt[...], preferred_element_type=F32) + dtb[...])
    dtT_o[...] = _softplus(
        lax.dot_general(wdtT[...], xb, (((1,), (1,)), ((), ())), preferred_element_type=F32)
        + dtbT[...])
    _proj_into(xb, wq, q_o, D_INNER_B, rope)
    _proj_into(xb, wk, k_o, D_INNER_B, rope)
    _proj_into(xb, wv, v_o, D_INNER_B)
    _proj_into(xb, wg, g_o, D_INNER_B)


def _inproj_even(x, tables, w_in, dt_bias, tm):
    m = x.shape[0]
    nt = tables[0].shape[0] // tm
    o = [0]
    for s in (D_INNER_A, CONV_DIM, H_A, D_INNER_B, D_INNER_B, D_INNER_B, D_INNER_B):
        o.append(o[-1] + s)
    wb = w_in.astype(BF16)
    wz, wxbc, wdt, wq, wk, wv, wg = [wb[:, o[i]:o[i + 1]] for i in range(7)]
    wdtT = wdt.T
    dtb = dt_bias.astype(F32).reshape(1, H_A)
    dtbT = dt_bias.astype(F32).reshape(H_A, 1)
    row = lambda w: pl.BlockSpec((tm, w), lambda i: (i, 0))
    tab = pl.BlockSpec((tm, LANES), lambda i: (i % nt, 0))
    consts = [wz, wxbc, wdt, wdtT, wq, wk, wv, wg, dtb, dtbT]
    out_shape = [jax.ShapeDtypeStruct((m, D_INNER_A), F32), jax.ShapeDtypeStruct((m, CONV_DIM), F32),
                 jax.ShapeDtypeStruct((m, H_A), F32), jax.ShapeDtypeStruct((H_A, m), F32)] + \
                [jax.ShapeDtypeStruct((m, D_INNER_B), F32)] * 4
    out_specs = [row(D_INNER_A), row(CONV_DIM), row(H_A), pl.BlockSpec((H_A, tm), lambda i: (0, i))] + \
                [row(D_INNER_B)] * 4
    return pl.pallas_call(
        _inproj_even_kernel, grid=(m // tm,),
        in_specs=[row(D_MODEL), tab, tab, tab] + [_const_spec(c.shape) for c in consts],
        out_specs=out_specs, out_shape=out_shape, compiler_params=_params(1),
        name="inproj_even")(x, *tables, *consts)


def _inproj_odd_kernel(x_ref, c_ref, s1_ref, s2_ref, wq, wk, wv, wg, q_o, k_o, v_o, g_o):
    xb = x_ref[...].astype(BF16)
    rope = (c_ref[...], s1_ref[...], s2_ref[...], ROT_C // 2)
    _proj_into(xb, wq, q_o, D_INNER_C, rope)
    _proj_into(xb, wk, k_o, D_INNER_C, rope)
    _proj_into(xb, wv, v_o, D_INNER_C)
    _proj_into(xb, wg, g_o, D_INNER_C)


def _inproj_odd(x, tables, w_in, tm):
    m = x.shape[0]
    nt = tables[0].shape[0] // tm
    wb = w_in.astype(BF16)
    ws = [wb[:, i * D_INNER_C:(i + 1) * D_INNER_C] for i in range(4)]
    row = lambda w: pl.BlockSpec((tm, w), lambda i: (i, 0))
    tab = pl.BlockSpec((tm, LANES), lambda i: (i % nt, 0))
    return pl.pallas_call(
        _inproj_odd_kernel, grid=(m // tm,),
        in_specs=[row(D_MODEL), tab, tab, tab] + [_const_spec(w.shape) for w in ws],
        out_specs=[row(D_INNER_C)] * 4,
        out_shape=[jax.ShapeDtypeStruct((m, D_INNER_C), F32)] * 4,
        compiler_params=_params(1), name="inproj_odd")(x, *tables, *ws)


def _post_kernel(n_y, *refs):
    y_refs = refs[:n_y]
    w_refs = refs[n_y:2 * n_y]
    x_ref, p_ref, g_ref, b_ref, wg_ref, wp_ref, o_ref = refs[2 * n_y:]
    out = _mm(y_refs[0][...], w_refs[0][...])
    for y, w in zip(y_refs[1:], w_refs[1:]):
        out = out + _mm(y[...], w[...])
    h = ALPHA * x_ref[...] + out
    hc = h - jnp.mean(h, axis=-1, keepdims=True)
    var = jnp.mean(hc * hc, axis=-1, keepdims=True)
    hn = hc * lax.rsqrt(var + EPS) * g_ref[...] + b_ref[...]
    gate = jax.nn.sigmoid(_mm(hn, wg_ref[...]))
    o_ref[...] = hn + gate * _mm(p_ref[...], wp_ref[...])


def _post(ys, w_outs, x, p, ln_g, ln_b, w_gate, w_proj, tm):
    m = x.shape[0]
    n_y = len(ys)
    row = lambda w: pl.BlockSpec((tm, w), lambda i: (i, 0))
    consts = [w.astype(BF16) for w in w_outs]
    tail = [ln_g.astype(F32).reshape(1, D_MODEL), ln_b.astype(F32).reshape(1, D_MODEL),
            w_gate.astype(BF16), w_proj.astype(BF16)]
    return pl.pallas_call(
        functools.partial(_post_kernel, n_y), grid=(m // tm,),
        in_specs=[row(y.shape[1]) for y in ys] + [_const_spec(c.shape) for c in consts]
                 + [row(D_MODEL), row(PLE_DIM)] + [_const_spec(c.shape) for c in tail],
        out_specs=row(D_MODEL), out_shape=jax.ShapeDtypeStruct((m, D_MODEL), F32),
        compiler_params=_params(1), name="outproj_post")(*ys, *consts, x, p, *tail)


def _gated_group_norm(y, z, nw):
    yz = y * _silu(z)
    gw = D_INNER_A // G_A
    outs = []
    for g in range(G_A):
        seg = yz[:, g * gw:(g + 1) * gw]
        ms = jnp.mean(seg * seg, axis=-1, keepdims=True)
        outs.append(seg * lax.rsqrt(ms + EPS) * nw[:, g * gw:(g + 1) * gw])
    return outs


def _ssd_prompt_kernel(z_ref, xbc_ref, dt_ref, dtT_ref, cw_ref, cb_ref, alog_ref, alogT_ref,
                       dexp_ref, nw_ref, ya_ref, st_ref, cbuf, hT):
    q = SSD_CHUNK
    c = pl.program_id(1)

    @pl.when(c == 0)
    def _():
        hT[...] = jnp.zeros_like(hT)
        cbuf[0:8, :] = jnp.zeros((8, CONV_DIM), F32)

    @pl.when(c > 0)
    def _():
        cbuf[0:8, :] = cbuf[q:q + 8, :]

    cbuf[8:q + 8, :] = xbc_ref[...]
    acc = cbuf[8:q + 8, :] * cw_ref[CONV_W - 1:CONV_W, :] + cb_ref[...]
    for s in range(1, CONV_W):
        acc = acc + cbuf[8 - s:q + 8 - s, :] * cw_ref[CONV_W - 1 - s:CONV_W - s, :]
    xbc = _silu(acc)
    xs = xbc[:, :D_INNER_A]
    bm = xbc[:, D_INNER_A:D_INNER_A + G_A * N_A]
    cm = xbc[:, D_INNER_A + G_A * N_A:]

    dt = dt_ref[...]
    dtT = dtT_ref[...]
    a = -jnp.exp(alog_ref[...])
    aT = -jnp.exp(alogT_ref[...])
    ri = lax.broadcasted_iota(jnp.int32, (q, q), 0)
    ci = lax.broadcasted_iota(jnp.int32, (q, q), 1)
    causal = ri >= ci
    acs = _mm_sel_lhs(jnp.where(causal, 1.0, 0.0), dt * a)
    acsT = _mm_sel_rhs(dtT * aT, jnp.where(ri <= ci, 1.0, 0.0))
    acs_end = acs[q - 1:q, :]
    e_hp = _seg_matrix(H_A, D_INNER_A, P_A, transpose=True)
    w_exp = _mm_sel_rhs(jnp.exp(acs_end - acs) * dt, e_hp)
    eacs_exp = _mm_sel_rhs(jnp.exp(acs), e_hp)
    cd_exp = _row_sel(jnp.exp(acs_end), e_hp)
    xw = xs * w_exp

    lane = lax.broadcasted_iota(jnp.int32, (q, LANES), 1)
    rep = H_A // G_A
    cb_g, bT_g = [], []
    for g in range(G_A):
        b_g = bm[:, g * N_A:(g + 1) * N_A]
        c_g = cm[:, g * N_A:(g + 1) * N_A]
        cb_g.append(_mm_nt(c_g, b_g))
        bT_g.append(b_g.T)

    for k in range(H_A // 2):
        g = (2 * k) // rep
        sl = slice(k * LANES, (k + 1) * LANES)
        ws = []
        for hh in (2 * k, 2 * k + 1):
            seg = acs[:, hh:hh + 1] - acsT[hh:hh + 1, :]
            dec = jnp.exp(jnp.where(causal, seg, NEG_INF))
            ws.append((cb_g[g] * dec * dtT[hh:hh + 1, :]).astype(BF16))
        xp = xs[:, sl]
        rhs = jnp.concatenate([jnp.where(lane < P_A, xp, 0.0), jnp.where(lane >= P_A, xp, 0.0)], axis=0)
        y_diag = jnp.dot(jnp.concatenate(ws, axis=1), rhs.astype(BF16), preferred_element_type=F32)
        h_in = hT[:, sl]
        y_off = _mm(cm[:, g * N_A:(g + 1) * N_A], h_in) * eacs_exp[:, sl]
        ya_ref[:, sl] = y_diag + y_off + dexp_ref[:, sl] * xp
        hT[:, sl] = h_in * cd_exp[:, sl] + _mm(bT_g[g], xw[:, sl])

    outs = _gated_group_norm(ya_ref[...], z_ref[...], nw_ref[...])
    gw = D_INNER_A // G_A
    for g in range(G_A):
        ya_ref[:, g * gw:(g + 1) * gw] = outs[g]

    @pl.when(c == pl.num_programs(1) - 1)
    def _():
        st_ref[...] = hT[...]


def _ssd_prompt(z, xbc, dt, dtT, conv_w, conv_b, a_log, d_skip, norm_w, batch, length):
    q = SSD_CHUNK
    nc = length // q
    blk = lambda w: pl.BlockSpec((q, w), lambda b, c: (b * nc + c, 0))
    consts = [conv_w.astype(F32), conv_b.astype(F32).reshape(1, CONV_DIM),
              a_log.astype(F32).reshape(1, H_A), a_log.astype(F32).reshape(H_A, 1),
              jnp.repeat(d_skip.astype(F32), P_A).reshape(1, D_INNER_A),
              norm_w.astype(F32).reshape(1, D_INNER_A)]
    ya, st = pl.pallas_call(
        _ssd_prompt_kernel, grid=(batch, nc),
        in_specs=[blk(D_INNER_A), blk(CONV_DIM), blk(H_A),
                  pl.BlockSpec((H_A, q), lambda b, c: (0, b * nc + c))]
                 + [_const_spec(cc.shape) for cc in consts],
        out_specs=[blk(D_INNER_A), pl.BlockSpec((None, N_A, D_INNER_A), lambda b, c: (b, 0, 0))],
        out_shape=[jax.ShapeDtypeStruct((batch * length, D_INNER_A), F32),
                   jax.ShapeDtypeStruct((batch, N_A, D_INNER_A), F32)],
        scratch_shapes=[pltpu.VMEM((q + 8, CONV_DIM), F32), pltpu.VMEM((N_A, D_INNER_A), F32)],
        compiler_params=_params(2), name="ssd_prompt")(z, xbc, dt, dtT, *consts)
    state = st.reshape(batch, N_A, H_A, P_A).transpose(0, 2, 3, 1)
    return ya, state


def _dec_conv_kernel(c0_ref, xbc_ref, cw_ref, cb_ref, o_ref):
    acc = xbc_ref[...] * cw_ref[CONV_W - 1:CONV_W, :] + cb_ref[...]
    for tap in range(CONV_W - 1):
        acc = acc + c0_ref[tap] * cw_ref[tap:tap + 1, :]
    o_ref[...] = _silu(acc)


def _dec_state_kernel(h_ref, x_ref, b_ref, c_ref, dt_ref, alog_ref, hn_ref, y_ref):
    dt = dt_ref[...]
    decay = jnp.exp(dt * (-jnp.exp(alog_ref[...])))
    hn = h_ref[...] * decay + (dt * x_ref[...]) * b_ref[...]
    hn_ref[...] = hn
    y_ref[...] = jnp.sum(hn * c_ref[...], axis=-1, keepdims=True)


def _dec_norm_kernel(y_ref, xs_ref, z_ref, dexp_ref, nw_ref, o_ref):
    y = y_ref[...] + dexp_ref[...] * xs_ref[...]
    outs = _gated_group_norm(y, z_ref[...], nw_ref[...])
    gw = D_INNER_A // G_A
    for g in range(G_A):
        o_ref[:, g * gw:(g + 1) * gw] = outs[g]


def _ssd_decode(z, xbc, dt, state0, conv0, conv_w, conv_b, a_log, d_skip, norm_w):
    s = z.shape[0]
    full = lambda shape: pl.BlockSpec(shape, lambda *_: (0,) * len(shape))
    c0t = conv0.transpose(1, 0, 2)
    cw = conv_w.astype(F32)
    cb = conv_b.astype(F32).reshape(1, CONV_DIM)
    act = pl.pallas_call(
        _dec_conv_kernel, grid=(1,),
        in_specs=[full(c0t.shape), full(xbc.shape), full(cw.shape), full(cb.shape)],
        out_specs=full(xbc.shape), out_shape=jax.ShapeDtypeStruct(xbc.shape, F32),
        compiler_params=_params(1), name="ssd_decode_conv")(c0t, xbc, cw, cb)
    xs = act[:, :D_INNER_A]
    rep = H_A // G_A
    x4 = xs.reshape(s, H_A, P_A, 1)
    b4 = jnp.repeat(act[:, D_INNER_A:D_INNER_A + G_A * N_A].reshape(s, G_A, 1, N_A), rep, axis=1)
    c4 = jnp.repeat(act[:, D_INNER_A + G_A * N_A:].reshape(s, G_A, 1, N_A), rep, axis=1)
    dt4 = dt.reshape(s, H_A, 1, 1)
    alog4 = a_log.astype(F32).reshape(1, H_A, 1, 1)
    ts = 8
    seq = lambda a, b: pl.BlockSpec((ts, H_A, a, b), lambda i: (i, 0, 0, 0))
    hn, y4 = pl.pallas_call(
        _dec_state_kernel, grid=(s // ts,),
        in_specs=[seq(P_A, N_A), seq(P_A, 1), seq(1, N_A), seq(1, N_A), seq(1, 1),
                  pl.BlockSpec((1, H_A, 1, 1), lambda i: (0, 0, 0, 0))],
        out_specs=[seq(P_A, N_A), seq(P_A, 1)],
        out_shape=[jax.ShapeDtypeStruct((s, H_A, P_A, N_A), F32),
                   jax.ShapeDtypeStruct((s, H_A, P_A, 1), F32)],
        compiler_params=_params(1), name="ssd_decode_state")(state0, x4, b4, c4, dt4, alog4)
    y = y4.reshape(s, D_INNER_A)
    dexp = jnp.repeat(d_skip.astype(F32), P_A).reshape(1, D_INNER_A)
    nw = norm_w.astype(F32).reshape(1, D_INNER_A)
    ya = pl.pallas_call(
        _dec_norm_kernel, grid=(1,),
        in_specs=[full(y.shape), full(xs.shape), full(z.shape), full(dexp.shape), full(nw.shape)],
        out_specs=full(y.shape), out_shape=jax.ShapeDtypeStruct(y.shape, F32),
        compiler_params=_params(1), name="ssd_decode_norm")(y, xs, z, dexp, nw)
    conv_new = jnp.concatenate([conv0[:, 1:], xbc[:, None, :]], axis=1)
    return ya, hn, conv_new


def _flash_update(carry, s, vb):
    m, l, acc = carry
    m_new = jnp.maximum(m, jnp.max(s, axis=1, keepdims=True))
    alpha = jnp.exp(m - m_new)
    p = jnp.exp(s - m_new)
    l = alpha * l + jnp.sum(p, axis=1, keepdims=True)
    acc = alpha * acc + jnp.dot(p.astype(BF16), vb, preferred_element_type=F32)
    return m_new, l, acc


def _flash_init(rows, width):
    return (jnp.full((rows, 1), NEG_INF, F32), jnp.zeros((rows, 1), F32), jnp.zeros((rows, width), F32))


def _moba_prompt_kernel(nb, q_ref, k_ref, v_ref, g_ref, o_ref, kmean):
    t = ATTN_TILE
    qi = pl.program_id(2)
    scale = HD_B ** -0.5

    @pl.when(qi == 0)
    def _():
        kmean[...] = jnp.zeros_like(kmean)

    q = q_ref[...]
    qb = q.astype(BF16)
    own = pl.multiple_of(qi * t, t)
    k_own = k_ref[pl.ds(own, t), :]
    ri = lax.broadcasted_iota(jnp.int32, (t, t), 0)
    ci = lax.broadcasted_iota(jnp.int32, (t, t), 1)
    s = jnp.where(ri >= ci, _mm_nt(qb, k_own) * scale, NEG_INF)
    carry = _flash_update(_flash_init(t, HD_B), s, v_ref[pl.ds(own, t), :].astype(BF16))

    col = lax.broadcasted_iota(jnp.int32, (t, nb), 1)
    gate = jnp.where(col < qi, _mm_nt_hp(q, kmean[...]), NEG_INF)
    sel = jnp.zeros((t, nb), F32)
    for _ in range(MOBA_TOPK):
        mx = jnp.max(gate, axis=1, keepdims=True)
        first = jnp.min(jnp.where(gate == mx, col, nb), axis=1, keepdims=True)
        real = jnp.where(mx > NEG_INF, 1.0, 0.0)
        sel = jnp.maximum(sel, jnp.where(col == first, real, 0.0))
        gate = jnp.where(col == first, NEG_INF, gate)

    def body(j, carry):
        start = pl.multiple_of(j * t, t)
        chosen = jnp.sum(jnp.where(col == j, sel, 0.0), axis=1, keepdims=True)
        sj = jnp.where(chosen > 0.0, _mm_nt(qb, k_ref[pl.ds(start, t), :]) * scale, NEG_INF)
        return _flash_update(carry, sj, v_ref[pl.ds(start, t), :].astype(BF16))

    m, l, acc = lax.fori_loop(0, qi, body, carry)
    o_ref[...] = (acc / l) * _silu(g_ref[...])

    rowb = lax.broadcasted_iota(jnp.int32, (nb, HD_B), 0)
    kmean[...] = jnp.where(rowb == qi, jnp.mean(k_own, axis=0, keepdims=True), kmean[...])


def _moba_prompt(q, k, v, g, batch, length):
    t = ATTN_TILE
    nq = length // t
    blk = pl.BlockSpec((t, HD_B), lambda b, h, i: (b * nq + i, h))
    seq = pl.BlockSpec((length, HD_B), lambda b, h, i: (b, h))
    return pl.pallas_call(
        functools.partial(_moba_prompt_kernel, nq), grid=(batch, H_B, nq),
        in_specs=[blk, seq, seq, blk], out_specs=blk,
        out_shape=jax.ShapeDtypeStruct((batch * length, D_INNER_B), F32),
        scratch_shapes=[pltpu.VMEM((nq, HD_B), F32)],
        compiler_params=_params(3), name="moba_prompt")(q, k, v, g)


def _lambda(lq1, lk1, lq2, lk2, lam_init):
    return (jnp.exp(jnp.sum(lq1[...] * lk1[...], axis=-1, keepdims=True))
            - jnp.exp(jnp.sum(lq2[...] * lk2[...], axis=-1, keepdims=True)) + lam_init)


def _diff_prompt_kernel(lam_init, q_ref, k_ref, v_ref, g_ref, lq1, lk1, lq2, lk2, sw_ref, o_ref):
    t = ATTN_TILE
    qi = pl.program_id(2)
    scale = DK_C ** -0.5
    lam = _lambda(lq1, lk1, lq2, lk2, lam_init)
    q = q_ref[...]
    lane = lax.broadcasted_iota(jnp.int32, (t, DV_C), 1)
    q1 = jnp.where(lane < DK_C, q, 0.0).astype(BF16)
    q2 = jnp.where(lane >= DK_C, q, 0.0).astype(BF16)

    def scores(start):
        kb = k_ref[pl.ds(start, t), :].astype(BF16)
        return _mm_nt(q1, kb) * scale, _mm_nt(q2, kb) * scale

    def body(j, carry):
        c1, c2 = carry
        start = pl.multiple_of(j * t, t)
        s1, s2 = scores(start)
        vb = v_ref[pl.ds(start, t), :].astype(BF16)
        return _flash_update(c1, s1, vb), _flash_update(c2, s2, vb)

    c1, c2 = lax.fori_loop(0, qi, body, (_flash_init(t, DV_C), _flash_init(t, DV_C)))
    own = pl.multiple_of(qi * t, t)
    ri = lax.broadcasted_iota(jnp.int32, (t, t), 0)
    ci = lax.broadcasted_iota(jnp.int32, (t, t), 1)
    s1, s2 = scores(own)
    vb = v_ref[pl.ds(own, t), :].astype(BF16)
    m1, l1, a1 = _flash_update(c1, jnp.where(ri >= ci, s1, NEG_INF), vb)
    m2, l2, a2 = _flash_update(c2, jnp.where(ri >= ci, s2, NEG_INF), vb)
    o = a1 / l1 - lam * (a2 / l2)
    o = o * lax.rsqrt(jnp.mean(o * o, axis=-1, keepdims=True) + EPS) * sw_ref[...] * (1.0 - lam_init)
    o_ref[...] = o * _silu(g_ref[...])


def _diff_prompt(q, k, v, g, lams, subln_w, lam_init, batch, length):
    t = ATTN_TILE
    nq = length // t
    blk = pl.BlockSpec((t, DV_C), lambda b, h, i: (b * nq + i, h))
    seq = pl.BlockSpec((length, DV_C), lambda b, h, i: (b, h))
    consts = [x.astype(F32).reshape(1, DK_C) for x in lams] + [subln_w.astype(F32).reshape(1, DV_C)]
    return pl.pallas_call(
        functools.partial(_diff_prompt_kernel, lam_init), grid=(batch, H_C, nq),
        in_specs=[blk, seq, seq, blk] + [_const_spec(c.shape) for c in consts], out_specs=blk,
        out_shape=jax.ShapeDtypeStruct((batch * length, D_INNER_C), F32),
        compiler_params=_params(3), name="diff_prompt")(q, k, v, g, *consts)


def _page_partial(q_row, k_ref, v_ref, e_seg, e_maps, scale):
    kk = k_ref[...]
    vv = v_ref[...]
    s = _mm_sel_rhs(kk * q_row, e_seg, parts=2) * scale
    m = jnp.max(s, axis=0, keepdims=True)
    p = jnp.exp(s - m)
    l = jnp.sum(p, axis=0, keepdims=True)
    pb = p.astype(BF16)
    accs = [jnp.sum(jnp.dot(pb, e.astype(BF16), preferred_element_type=F32) * vv, axis=0, keepdims=True)
            for e in e_maps]
    return m, l, accs, jnp.sum(kk, axis=0, keepdims=True)


def _combine(ms, ls, accs, valid, s_self, v_row, e_maps):
    m_all = jnp.concatenate(ms, axis=0)
    l_all = jnp.concatenate(ls, axis=0)
    if valid is None:
        mtot = jnp.maximum(jnp.max(m_all, axis=0, keepdims=True), s_self)
        w = jnp.exp(m_all - mtot)
    else:
        mtot = jnp.maximum(jnp.max(jnp.where(valid > 0.0, m_all, NEG_INF), axis=0, keepdims=True), s_self)
        w = jnp.where(valid > 0.0, jnp.exp(jnp.minimum(m_all - mtot, 0.0)), 0.0)
    p_self = jnp.exp(s_self - mtot)
    l_tot = jnp.sum(w * l_all, axis=0, keepdims=True) + p_self
    outs = []
    for mi, e in enumerate(e_maps):
        a_all = jnp.concatenate([a[mi] for a in accs], axis=0)
        tot = jnp.sum(_mm_sel_rhs(w, e) * a_all, axis=0, keepdims=True) + _row_sel(p_self, e) * v_row
        outs.append(tot / _row_sel(l_tot, e))
    return outs


def _moba_decode_kernel(n_pages, pt_ref, q_ref, kn_ref, vn_ref, g_ref, *refs):
    k_refs = refs[:n_pages]
    v_refs = refs[n_pages:2 * n_pages]
    o_ref = refs[2 * n_pages]
    scale = HD_B ** -0.5
    ppb = MOBA_BLOCK // PAGE_SIZE
    nb = n_pages // ppb
    e_seg = _seg_matrix(D_INNER_B, LANES, HD_B)
    e_map = _seg_matrix(LANES, D_INNER_B, HD_B, transpose=True)
    q_row = q_ref[0]
    ms, ls, accs, ksums = [], [], [], []
    for t in range(n_pages):
        m, l, a, ks = _page_partial(q_row, k_refs[t], v_refs[t], e_seg, [e_map], scale)
        ms.append(m); ls.append(l); accs.append(a); ksums.append(ks)
    kmean = jnp.concatenate([sum(ksums[b * ppb:(b + 1) * ppb]) for b in range(nb)], axis=0) * (1.0 / MOBA_BLOCK)
    gate = _mm_sel_rhs(kmean * q_row, e_seg)
    rowi = lax.broadcasted_iota(jnp.int32, (nb, LANES), 0)
    sel = jnp.zeros((nb, LANES), F32)
    for _ in range(min(MOBA_TOPK, nb)):
        mx = jnp.max(gate, axis=0, keepdims=True)
        first = jnp.min(jnp.where(gate == mx, rowi, nb), axis=0, keepdims=True)
        pick = rowi == first
        sel = jnp.where(pick, 1.0, sel)
        gate = jnp.where(pick, NEG_INF, gate)
    valid = jnp.concatenate([sel[t // ppb:t // ppb + 1, :] for t in range(n_pages)], axis=0)
    s_self = _row_sel(q_row * kn_ref[0], e_seg) * scale
    (o,) = _combine(ms, ls, accs, valid, s_self, vn_ref[0], [e_map])
    o_ref[0] = o * _silu(g_ref[0])


def _diff_decode_kernel(n_pages, lam_init, pt_ref, q_ref, kn_ref, vn_ref, g_ref,
                        lq1, lk1, lq2, lk2, sw_ref, *refs):
    k_refs = refs[:n_pages]
    v_refs = refs[n_pages:2 * n_pages]
    o_ref = refs[2 * n_pages]
    scale = DK_C ** -0.5
    lam = _lambda(lq1, lk1, lq2, lk2, lam_init)
    e_seg = _seg_matrix(D_INNER_C, LANES, DK_C)
    sid = lax.broadcasted_iota(jnp.int32, (LANES, D_INNER_C), 0)
    lane = lax.broadcasted_iota(jnp.int32, (LANES, D_INNER_C), 1)
    e_maps = [jnp.where(sid == (lane // DV_C) * 2 + mi, 1.0, 0.0).astype(F32) for mi in range(2)]
    q_row = q_ref[0]
    ms, ls, accs = [], [], []
    for t in range(n_pages):
        m, l, a, _ = _page_partial(q_row, k_refs[t], v_refs[t], e_seg, e_maps, scale)
        ms.append(m); ls.append(l); accs.append(a)
    s_self = _row_sel(q_row * kn_ref[0], e_seg) * scale
    o1, o2 = _combine(ms, ls, accs, None, s_self, vn_ref[0], e_maps)
    o = o1 - lam * o2
    e_head = _seg_matrix(D_INNER_C, LANES, DV_C)
    e_headT = _seg_matrix(LANES, D_INNER_C, DV_C, transpose=True)
    ms_head = _row_sel(o * o, e_head) * (1.0 / DV_C)
    o = o * _row_sel(lax.rsqrt(ms_head + EPS), e_headT) * sw_ref[...] * (1.0 - lam_init)
    o_ref[0] = o * _silu(g_ref[0])


def _decode_attn(kernel_fn, q, kn, vn, g, consts, cache_k, cache_v, li, pt_flat, n_pages, name):
    s, width = q.shape
    n_pool = cache_k.shape[1]
    ck = cache_k.reshape(cache_k.shape[0], n_pool, PAGE_SIZE, width)
    cv = cache_v.reshape(cache_v.shape[0], n_pool, PAGE_SIZE, width)
    r3 = lambda a: a.reshape(s, 1, width)
    row = pl.BlockSpec((1, 1, width), lambda i, pt: (i, 0, 0))

    def page(t):
        return pl.BlockSpec((None, None, PAGE_SIZE, width), lambda i, pt: (li, pt[i * n_pages + t], 0, 0))

    const_specs = [pl.BlockSpec(c.shape, lambda i, pt, nd=c.ndim: (0,) * nd) for c in consts]
    grid_spec = pltpu.PrefetchScalarGridSpec(
        num_scalar_prefetch=1, grid=(s,),
        in_specs=[row, row, row, row] + const_specs + [page(t) for t in range(n_pages)] * 2,
        out_specs=row)
    out = pl.pallas_call(
        kernel_fn, grid_spec=grid_spec, out_shape=jax.ShapeDtypeStruct((s, 1, width), F32),
        compiler_params=_params(1), name=name)(
            pt_flat, r3(q), r3(kn), r3(vn), r3(g), *consts, *([ck] * n_pages), *([cv] * n_pages))
    return out.reshape(s, width)


def _rope_tables(pos, rot_dim, period):
    half = rot_dim // 2
    n = pos.shape[0]
    inv = ROPE_THETA ** (-jnp.arange(half, dtype=F32) * (2.0 / rot_dim))
    ang = pos.astype(F32)[:, None] * inv[None, :]
    cos, sin = jnp.cos(ang), jnp.sin(ang)
    c = jnp.concatenate([cos, cos, jnp.ones((n, period - rot_dim), F32)], axis=1)
    s1 = jnp.concatenate([-sin, jnp.zeros((n, period - half), F32)], axis=1)
    s2 = jnp.concatenate([jnp.zeros((n, half), F32), sin, jnp.zeros((n, period - rot_dim), F32)], axis=1)
    reps = LANES // period
    return tuple(jnp.tile(a, (1, reps)) for a in (c, s1, s2))


def kernel(x_prompt, x_sample, cache_moba_k, cache_moba_v, cache_diff_k, cache_diff_v, state_ssm, state_conv, page_table, p_prompt, p_sample, w_in_even, conv_w, conv_b, dt_bias, a_log, d_skip, ssm_norm_w, w_out_even, w_in_odd, lambda_q1, lambda_k1, lambda_q2, lambda_k2, subln_w, w_out_odd, ln_g, ln_b, w_ple_gate, w_ple_proj):
    bp, length, _ = x_prompt.shape
    s = x_sample.shape[0]
    assert x_sample.shape[1] == 1
    n_pages = page_table.shape[1]
    past_len = n_pages * PAGE_SIZE
    assert past_len % MOBA_BLOCK == 0 and length % ATTN_TILE == 0 and length % SSD_CHUNK == 0
    assert s % 8 == 0
    pt_flat = page_table.reshape(-1).astype(jnp.int32)
    tm_p = 256
    xp = x_prompt.reshape(bp * length, D_MODEL)
    xs = x_sample.reshape(s, D_MODEL)
    pos_p = jnp.arange(length, dtype=jnp.int32)
    pos_s = jnp.full((s,), past_len, jnp.int32)
    tabs = {("b", "p"): _rope_tables(pos_p, ROT_B, HD_B), ("b", "s"): _rope_tables(pos_s, ROT_B, HD_B),
            ("c", "p"): _rope_tables(pos_p, ROT_C, DK_C), ("c", "s"): _rope_tables(pos_s, ROT_C, DK_C)}

    pm_k, pm_v, pd_k, pd_v, p_ssm, p_conv = [], [], [], [], [], []
    sm_k, sm_v, sd_k, sd_v, s_ssm, s_conv = [], [], [], [], [], []
    for i in range(DEPTH):
        li = i // 2
        post = functools.partial(_post, ln_g=ln_g[i], ln_b=ln_b[i], w_gate=w_ple_gate[i], w_proj=w_ple_proj[i])
        pp = p_prompt[i].reshape(bp * length, PLE_DIM)
        ps = p_sample[i].reshape(s, PLE_DIM)
        if i % 2 == 0:
            ssm_w = (conv_w[li], conv_b[li], a_log[li], d_skip[li], ssm_norm_w[li])
            w_outs = [w_out_even[li][:D_INNER_A], w_out_even[li][D_INNER_A:]]
            z, xbc, dt, dtT, q, k, v, g = _inproj_even(xp, tabs[("b", "p")], w_in_even[li], dt_bias[li], tm_p)
            ya, st = _ssd_prompt(z, xbc, dt, dtT, *ssm_w, bp, length)
            yb = _moba_prompt(q, k, v, g, bp, length)
            xp = post([ya, yb], w_outs, xp, pp, tm=tm_p)
            pm_k.append(k.reshape(bp, length, H_B, HD_B)); pm_v.append(v.reshape(bp, length, H_B, HD_B))
            p_ssm.append(st)
            p_conv.append(xbc.reshape(bp, length, CONV_DIM)[:, length - (CONV_W - 1):])
            z, xbc, dt, dtT, q, k, v, g = _inproj_even(xs, tabs[("b", "s")], w_in_even[li], dt_bias[li], s)
            ya, st, cv = _ssd_decode(z, xbc, dt, state_ssm[li], state_conv[li], *ssm_w)
            yb = _decode_attn(functools.partial(_moba_decode_kernel, n_pages), q, k, v, g, [],
                              cache_moba_k, cache_moba_v, li, pt_flat, n_pages, "moba_decode")
            xs = post([ya, yb], w_outs, xs, ps, tm=s)
            sm_k.append(k.reshape(s, 1, H_B, HD_B)); sm_v.append(v.reshape(s, 1, H_B, HD_B))
            s_ssm.append(st); s_conv.append(cv)
        else:
            lam_init = 0.8 - 0.6 * math.exp(-0.3 * i)
            lams = (lambda_q1[li], lambda_k1[li], lambda_q2[li], lambda_k2[li])
            q, k, v, g = _inproj_odd(xp, tabs[("c", "p")], w_in_odd[li], tm_p)
            y = _diff_prompt(q, k, v, g, lams, subln_w[li], lam_init, bp, length)
            xp = post([y], [w_out_odd[li]], xp, pp, tm=tm_p)
            pd_k.append(k.reshape(bp, length, H_C, 2 * DK_C)); pd_v.append(v.reshape(bp, length, H_C, DV_C))
            q, k, v, g = _inproj_odd(xs, tabs[("c", "s")], w_in_odd[li], s)
            consts = [x.astype(F32).reshape(1, DK_C) for x in lams] + \
                     [jnp.tile(subln_w[li].astype(F32), H_C).reshape(1, D_INNER_C)]
            y = _decode_attn(functools.partial(_diff_decode_kernel, n_pages, lam_init), q, k, v, g, consts,
                             cache_diff_k, cache_diff_v, li, pt_flat, n_pages, "diff_decode")
            xs = post([y], [w_out_odd[li]], xs, ps, tm=s)
            sd_k.append(k.reshape(s, 1, H_C, 2 * DK_C)); sd_v.append(v.reshape(s, 1, H_C, DV_C))
    return (xp.reshape(bp, length, D_MODEL), xs.reshape(s, 1, D_MODEL),
            jnp.stack(pm_k), jnp.stack(pm_v), jnp.stack(pd_k), jnp.stack(pd_v),
            jnp.stack(p_ssm), jnp.stack(p_conv),
            jnp.stack(sm_k), jnp.stack(sm_v), jnp.stack(sd_k), jnp.stack(sd_v),
            jnp.stack(s_ssm), jnp.stack(s_conv))
```

```python
import functools
import math

import jax
import jax.numpy as jnp
from jax import lax
from jax.experimental import pallas as pl
from jax.experimental.pallas import tpu as pltpu

F32 = jnp.float32
BF16 = jnp.bfloat16

D_MODEL = 1024
DEPTH = 4
PAGE_SIZE = 128
PLE_DIM = 256
EPS = 1e-5
ROPE_THETA = 500000.0
ALPHA = (2 * DEPTH) ** 0.25

D_INNER_A = D_MODEL
P_A = 64
H_A = D_INNER_A // P_A
G_A = 2
N_A = 128
CONV_W = 4
CONV_DIM = D_INNER_A + 2 * G_A * N_A
SSD_CHUNK = 128

H_B = 8
HD_B = D_MODEL // H_B
D_INNER_B = H_B * HD_B
ROT_B = HD_B // 4
MOBA_BLOCK = 256
MOBA_TOPK = 3

H_C = 8
DK_C = D_MODEL // (2 * H_C)
DV_C = 2 * DK_C
D_INNER_C = H_C * DV_C
ROT_C = DK_C // 4

LANES = 128
ATTN_TILE = 256
ATTN_HEADS = 2
ATTN_CHUNK = 4
VMEM_LIMIT = 56 * 1024 * 1024
NEG_INF = float("-inf")
MASK_BIAS = -1e30


def _split2(a):
    hi = a.astype(BF16)
    lo = (a - hi.astype(F32)).astype(BF16)
    return hi, lo


def _split3(a):
    hi = a.astype(BF16)
    r = a - hi.astype(F32)
    mid = r.astype(BF16)
    lo = (r - mid.astype(F32)).astype(BF16)
    return hi, mid, lo


def _mm(a, b):
    return jnp.dot(a.astype(BF16), b.astype(BF16), preferred_element_type=F32)


def _mm_nt(a, b):
    return lax.dot_general(a.astype(BF16), b.astype(BF16), (((1,), (1,)), ((), ())),
                           preferred_element_type=F32)


def _mm_sel_rhs(a, e, parts=3):
    eb = e.astype(BF16)
    pieces = _split3(a) if parts == 3 else _split2(a)
    out = jnp.dot(pieces[0], eb, preferred_element_type=F32)
    for p in pieces[1:]:
        out = out + jnp.dot(p, eb, preferred_element_type=F32)
    return out


def _mm_sel_lhs(e, b):
    eb = e.astype(BF16)
    pieces = _split3(b)
    out = jnp.dot(eb, pieces[0], preferred_element_type=F32)
    for p in pieces[1:]:
        out = out + jnp.dot(eb, p, preferred_element_type=F32)
    return out


def _mm_nt_hp(a, b):
    ah, al = _split2(a)
    bh, bl = _split2(b)
    dn = (((1,), (1,)), ((), ()))
    return (lax.dot_general(ah, bh, dn, preferred_element_type=F32)
            + lax.dot_general(ah, bl, dn, preferred_element_type=F32)
            + lax.dot_general(al, bh, dn, preferred_element_type=F32))


def _rows8(row):
    return jnp.broadcast_to(row, (8, row.shape[1]))


def _row_sel(row, e):
    return _mm_sel_rhs(_rows8(row), e)[0:1, :]


def _silu(x):
    return x * jax.nn.sigmoid(x)


def _softplus(x):
    return jnp.maximum(x, 0.0) + jnp.log1p(jnp.exp(-jnp.abs(x)))


def _rope128(x, c, s1, s2, r):
    return x * c + pltpu.roll(x, LANES - r, 1) * s1 + pltpu.roll(x, r, 1) * s2


def _seg_matrix(n_rows, n_cols, seg, transpose=False):
    if not transpose:
        lane = lax.broadcasted_iota(jnp.int32, (n_rows, n_cols), 0)
        sid = lax.broadcasted_iota(jnp.int32, (n_rows, n_cols), 1)
    else:
        sid = lax.broadcasted_iota(jnp.int32, (n_rows, n_cols), 0)
        lane = lax.broadcasted_iota(jnp.int32, (n_rows, n_cols), 1)
    return jnp.where(lane // seg == sid, 1.0, 0.0).astype(F32)


def _const_spec(shape):
    nd = len(shape)
    return pl.BlockSpec(shape, lambda *_: (0,) * nd, pipeline_mode=pl.Buffered(1))


def _params(n_grid):
    return pltpu.CompilerParams(dimension_semantics=("arbitrary",) * n_grid,
                                vmem_limit_bytes=VMEM_LIMIT)


_PROJ_CHUNK = 512


def _proj_into(xb, w_ref, o_ref, width, rope=None):
    for c0 in range(0, width, _PROJ_CHUNK):
        cw = min(_PROJ_CHUNK, width - c0)
        r = jnp.dot(xb, w_ref[:, c0:c0 + cw], preferred_element_type=F32)
        if rope is None:
            o_ref[:, c0:c0 + cw] = r
        else:
            c, s1, s2, shift = rope
            for j in range(cw // LANES):
                o_ref[:, c0 + j * LANES:c0 + (j + 1) * LANES] = _rope128(
                    r[:, j * LANES:(j + 1) * LANES], c, s1, s2, shift)


def _inproj_even_kernel(x_ref, c_ref, s1_ref, s2_ref, wz, wxbc, wdt, wdtT, wq, wk, wv, wg,
                        dtb, dtbT, z_o, xbc_o, dt_o, dtT_o, q_o, k_o, v_o, g_o):
    xb = x_ref[...].astype(BF16)
    rope = (c_ref[...], s1_ref[...], s2_ref[...], ROT_B // 2)
    _proj_into(xb, wz, z_o, D_INNER_A)
    _proj_into(xb, wxbc, xbc_o, CONV_DIM)
    dt_o[...] = _softplus(jnp.dot(xb, wdt[...], preferred_element_type=F32) + dtb[...])
    dtT_o[...] = _softplus(
        lax.dot_general(wdtT[...], xb, (((1,), (1,)), ((), ())), preferred_element_type=F32)
        + dtbT[...])
    _proj_into(xb, wq, q_o, D_INNER_B, rope)
    _proj_into(xb, wk, k_o, D_INNER_B, rope)
    _proj_into(xb, wv, v_o, D_INNER_B)
    _proj_into(xb, wg, g_o, D_INNER_B)


def _inproj_even(x, tables, w_in, dt_bias, tm):
    m = x.shape[0]
    nt = tables[0].shape[0] // tm
    o = [0]
    for s in (D_INNER_A, CONV_DIM, H_A, D_INNER_B, D_INNER_B, D_INNER_B, D_INNER_B):
        o.append(o[-1] + s)
    wb = w_in.astype(BF16)
    wz, wxbc, wdt, wq, wk, wv, wg = [wb[:, o[i]:o[i + 1]] for i in range(7)]
    wdtT = wdt.T
    dtb = dt_bias.astype(F32).reshape(1, H_A)
    dtbT = dt_bias.astype(F32).reshape(H_A, 1)
    row = lambda w: pl.BlockSpec((tm, w), lambda i: (i, 0))
    tab = pl.BlockSpec((tm, LANES), lambda i: (i % nt, 0))
    consts = [wz, wxbc, wdt, wdtT, wq, wk, wv, wg, dtb, dtbT]
    out_shape = [jax.ShapeDtypeStruct((m, D_INNER_A), F32), jax.ShapeDtypeStruct((m, CONV_DIM), F32),
                 jax.ShapeDtypeStruct((m, H_A), F32), jax.ShapeDtypeStruct((H_A, m), F32)] + \
                [jax.ShapeDtypeStruct((m, D_INNER_B), F32)] * 4
    out_specs = [row(D_INNER_A), row(CONV_DIM), row(H_A), pl.BlockSpec((H_A, tm), lambda i: (0, i))] + \
                [row(D_INNER_B)] * 4
    return pl.pallas_call(
        _inproj_even_kernel, grid=(m // tm,),
        in_specs=[row(D_MODEL), tab, tab, tab] + [_const_spec(c.shape) for c in consts],
        out_specs=out_specs, out_shape=out_shape, compiler_params=_params(1),
        name="inproj_even")(x, *tables, *consts)


def _inproj_odd_kernel(x_ref, c_ref, s1_ref, s2_ref, wq, wk, wv, wg, q_o, k_o, v_o, g_o):
    xb = x_ref[...].astype(BF16)
    rope = (c_ref[...], s1_ref[...], s2_ref[...], ROT_C // 2)
    _proj_into(xb, wq, q_o, D_INNER_C, rope)
    _proj_into(xb, wk, k_o, D_INNER_C, rope)
    _proj_into(xb, wv, v_o, D_INNER_C)
    _proj_into(xb, wg, g_o, D_INNER_C)


def _inproj_odd(x, tables, w_in, tm):
    m = x.shape[0]
    nt = tables[0].shape[0] // tm
    wb = w_in.astype(BF16)
    ws = [wb[:, i * D_INNER_C:(i + 1) * D_INNER_C] for i in range(4)]
    row = lambda w: pl.BlockSpec((tm, w), lambda i: (i, 0))
    tab = pl.BlockSpec((tm, LANES), lambda i: (i % nt, 0))
    return pl.pallas_call(
        _inproj_odd_kernel, grid=(m // tm,),
        in_specs=[row(D_MODEL), tab, tab, tab] + [_const_spec(w.shape) for w in ws],
        out_specs=[row(D_INNER_C)] * 4,
        out_shape=[jax.ShapeDtypeStruct((m, D_INNER_C), F32)] * 4,
        compiler_params=_params(1), name="inproj_odd")(x, *tables, *ws)


def _post_kernel(n_y, *refs):
    y_refs = refs[:n_y]
    w_refs = refs[n_y:2 * n_y]
    x_ref, p_ref, g_ref, b_ref, wg_ref, wp_ref, o_ref = refs[2 * n_y:]
    out = _mm(y_refs[0][...], w_refs[0][...])
    for y, w in zip(y_refs[1:], w_refs[1:]):
        out = out + _mm(y[...], w[...])
    h = ALPHA * x_ref[...] + out
    hc = h - jnp.mean(h, axis=-1, keepdims=True)
    var = jnp.mean(hc * hc, axis=-1, keepdims=True)
    hn = hc * lax.rsqrt(var + EPS) * g_ref[...] + b_ref[...]
    gate = jax.nn.sigmoid(_mm(hn, wg_ref[...]))
    o_ref[...] = hn + gate * _mm(p_ref[...], wp_ref[...])


def _post(ys, w_outs, x, p, ln_g, ln_b, w_gate, w_proj, tm):
    m = x.shape[0]
    n_y = len(ys)
    row = lambda w: pl.BlockSpec((tm, w), lambda i: (i, 0))
    consts = [w.astype(BF16) for w in w_outs]
    tail = [ln_g.astype(F32).reshape(1, D_MODEL), ln_b.astype(F32).reshape(1, D_MODEL),
            w_gate.astype(BF16), w_proj.astype(BF16)]
    return pl.pallas_call(
        functools.partial(_post_kernel, n_y), grid=(m // tm,),
        in_specs=[row(y.shape[1]) for y in ys] + [_const_spec(c.shape) for c in consts]
                 + [row(D_MODEL), row(PLE_DIM)] + [_const_spec(c.shape) for c in tail],
        out_specs=row(D_MODEL), out_shape=jax.ShapeDtypeStruct((m, D_MODEL), F32),
        compiler_params=_params(1), name="outproj_post")(*ys, *consts, x, p, *tail)


def _gated_group_norm(y, z, nw):
    yz = y * _silu(z)
    gw = D_INNER_A // G_A
    outs = []
    for g in range(G_A):
        seg = yz[:, g * gw:(g + 1) * gw]
        ms = jnp.mean(seg * seg, axis=-1, keepdims=True)
        outs.append(seg * lax.rsqrt(ms + EPS) * nw[:, g * gw:(g + 1) * gw])
    return outs


def _ssd_prompt_kernel(z_ref, xbc_ref, dt_ref, dtT_ref, cw_ref, cb_ref, alog_ref, alogT_ref,
                       dexp_ref, nw_ref, ya_ref, st_ref, cbuf, hT):
    q = SSD_CHUNK
    c = pl.program_id(1)

    @pl.when(c == 0)
    def _():
        hT[...] = jnp.zeros_like(hT)
        cbuf[0:8, :] = jnp.zeros((8, CONV_DIM), F32)

    @pl.when(c > 0)
    def _():
        cbuf[0:8, :] = cbuf[q:q + 8, :]

    cbuf[8:q + 8, :] = xbc_ref[...]
    acc = cbuf[8:q + 8, :] * cw_ref[CONV_W - 1:CONV_W, :] + cb_ref[...]
    for s in range(1, CONV_W):
        acc = acc + cbuf[8 - s:q + 8 - s, :] * cw_ref[CONV_W - 1 - s:CONV_W - s, :]
    xbc = _silu(acc)
    xs = xbc[:, :D_INNER_A]
    bm = xbc[:, D_INNER_A:D_INNER_A + G_A * N_A]
    cm = xbc[:, D_INNER_A + G_A * N_A:]

    dt = dt_ref[...]
    dtT = dtT_ref[...]
    a = -jnp.exp(alog_ref[...])
    aT = -jnp.exp(alogT_ref[...])
    ri = lax.broadcasted_iota(jnp.int32, (q, q), 0)
    ci = lax.broadcasted_iota(jnp.int32, (q, q), 1)
    causal = ri >= ci
    acs = _mm_sel_lhs(jnp.where(causal, 1.0, 0.0), dt * a)
    acsT = _mm_sel_rhs(dtT * aT, jnp.where(ri <= ci, 1.0, 0.0))
    acs_end = acs[q - 1:q, :]
    e_hp = _seg_matrix(H_A, D_INNER_A, P_A, transpose=True)
    w_exp = _mm_sel_rhs(jnp.exp(acs_end - acs) * dt, e_hp)
    eacs_exp = _mm_sel_rhs(jnp.exp(acs), e_hp)
    cd_exp = _row_sel(jnp.exp(acs_end), e_hp)
    xw = xs * w_exp

    lane = lax.broadcasted_iota(jnp.int32, (q, LANES), 1)
    rep = H_A // G_A
    cb_g, bT_g = [], []
    for g in range(G_A):
        b_g = bm[:, g * N_A:(g + 1) * N_A]
        c_g = cm[:, g * N_A:(g + 1) * N_A]
        cb_g.append(_mm_nt(c_g, b_g))
        bT_g.append(b_g.T)

    for k in range(H_A // 2):
        g = (2 * k) // rep
        sl = slice(k * LANES, (k + 1) * LANES)
        ws = []
        for hh in (2 * k, 2 * k + 1):
            seg = acs[:, hh:hh + 1] - acsT[hh:hh + 1, :]
            dec = jnp.exp(jnp.where(causal, seg, NEG_INF))
            ws.append((cb_g[g] * dec * dtT[hh:hh + 1, :]).astype(BF16))
        xp = xs[:, sl]
        rhs = jnp.concatenate([jnp.where(lane < P_A, xp, 0.0), jnp.where(lane >= P_A, xp, 0.0)], axis=0)
        y_diag = jnp.dot(jnp.concatenate(ws, axis=1), rhs.astype(BF16), preferred_element_type=F32)
        h_in = hT[:, sl]
        y_off = _mm(cm[:, g * N_A:(g + 1) * N_A], h_in) * eacs_exp[:, sl]
        ya_ref[:, sl] = y_diag + y_off + dexp_ref[:, sl] * xp
        hT[:, sl] = h_in * cd_exp[:, sl] + _mm(bT_g[g], xw[:, sl])

    outs = _gated_group_norm(ya_ref[...], z_ref[...], nw_ref[...])
    gw = D_INNER_A // G_A
    for g in range(G_A):
        ya_ref[:, g * gw:(g + 1) * gw] = outs[g]

    @pl.when(c == pl.num_programs(1) - 1)
    def _():
        st_ref[...] = hT[...]


def _ssd_prompt(z, xbc, dt, dtT, conv_w, conv_b, a_log, d_skip, norm_w, batch, length):
    q = SSD_CHUNK
    nc = length // q
    blk = lambda w: pl.BlockSpec((q, w), lambda b, c: (b * nc + c, 0))
    consts = [conv_w.astype(F32), conv_b.astype(F32).reshape(1, CONV_DIM),
              a_log.astype(F32).reshape(1, H_A), a_log.astype(F32).reshape(H_A, 1),
              jnp.repeat(d_skip.astype(F32), P_A).reshape(1, D_INNER_A),
              norm_w.astype(F32).reshape(1, D_INNER_A)]
    ya, st = pl.pallas_call(
        _ssd_prompt_kernel, grid=(batch, nc),
        in_specs=[blk(D_INNER_A), blk(CONV_DIM), blk(H_A),
                  pl.BlockSpec((H_A, q), lambda b, c: (0, b * nc + c))]
                 + [_const_spec(cc.shape) for cc in consts],
        out_specs=[blk(D_INNER_A), pl.BlockSpec((None, N_A, D_INNER_A), lambda b, c: (b, 0, 0))],
        out_shape=[jax.ShapeDtypeStruct((batch * length, D_INNER_A), F32),
                   jax.ShapeDtypeStruct((batch, N_A, D_INNER_A), F32)],
        scratch_shapes=[pltpu.VMEM((q + 8, CONV_DIM), F32), pltpu.VMEM((N_A, D_INNER_A), F32)],
        compiler_params=_params(2), name="ssd_prompt")(z, xbc, dt, dtT, *consts)
    state = st.reshape(batch, N_A, H_A, P_A).transpose(0, 2, 3, 1)
    return ya, state


def _dec_conv_kernel(c0_ref, xbc_ref, cw_ref, cb_ref, o_ref):
    acc = xbc_ref[...] * cw_ref[CONV_W - 1:CONV_W, :] + cb_ref[...]
    for tap in range(CONV_W - 1):
        acc = acc + c0_ref[tap] * cw_ref[tap:tap + 1, :]
    o_ref[...] = _silu(acc)


def _dec_state_kernel(h_ref, x_ref, b_ref, c_ref, dt_ref, alog_ref, hn_ref, y_ref):
    dt = dt_ref[...]
    decay = jnp.exp(dt * (-jnp.exp(alog_ref[...])))
    hn = h_ref[...] * decay + (dt * x_ref[...]) * b_ref[...]
    hn_ref[...] = hn
    y_ref[...] = jnp.sum(hn * c_ref[...], axis=-1, keepdims=True)


def _dec_norm_kernel(y_ref, xs_ref, z_ref, dexp_ref, nw_ref, o_ref):
    y = y_ref[...] + dexp_ref[...] * xs_ref[...]
    outs = _gated_group_norm(y, z_ref[...], nw_ref[...])
    gw = D_INNER_A // G_A
    for g in range(G_A):
        o_ref[:, g * gw:(g + 1) * gw] = outs[g]


def _ssd_decode(z, xbc, dt, state0, conv0, conv_w, conv_b, a_log, d_skip, norm_w):
    s = z.shape[0]
    full = lambda shape: pl.BlockSpec(shape, lambda *_: (0,) * len(shape))
    c0t = conv0.transpose(1, 0, 2)
    cw = conv_w.astype(F32)
    cb = conv_b.astype(F32).reshape(1, CONV_DIM)
    act = pl.pallas_call(
        _dec_conv_kernel, grid=(1,),
        in_specs=[full(c0t.shape), full(xbc.shape), full(cw.shape), full(cb.shape)],
        out_specs=full(xbc.shape), out_shape=jax.ShapeDtypeStruct(xbc.shape, F32),
        compiler_params=_params(1), name="ssd_decode_conv")(c0t, xbc, cw, cb)
    xs = act[:, :D_INNER_A]
    rep = H_A // G_A
    x4 = xs.reshape(s, H_A, P_A, 1)
    b4 = jnp.repeat(act[:, D_INNER_A:D_INNER_A + G_A * N_A].reshape(s, G_A, 1, N_A), rep, axis=1)
    c4 = jnp.repeat(act[:, D_INNER_A + G_A * N_A:].reshape(s, G_A, 1, N_A), rep, axis=1)
    dt4 = dt.reshape(s, H_A, 1, 1)
    alog4 = a_log.astype(F32).reshape(1, H_A, 1, 1)
    ts = 8
    seq = lambda a, b: pl.BlockSpec((ts, H_A, a, b), lambda i: (i, 0, 0, 0))
    hn, y4 = pl.pallas_call(
        _dec_state_kernel, grid=(s // ts,),
        in_specs=[seq(P_A, N_A), seq(P_A, 1), seq(1, N_A), seq(1, N_A), seq(1, 1),
                  pl.BlockSpec((1, H_A, 1, 1), lambda i: (0, 0, 0, 0))],
        out_specs=[seq(P_A, N_A), seq(P_A, 1)],
        out_shape=[jax.ShapeDtypeStruct((s, H_A, P_A, N_A), F32),
                   jax.ShapeDtypeStruct((s, H_A, P_A, 1), F32)],
        compiler_params=_params(1), name="ssd_decode_state")(state0, x4, b4, c4, dt4, alog4)
    y = y4.reshape(s, D_INNER_A)
    dexp = jnp.repeat(d_skip.astype(F32), P_A).reshape(1, D_INNER_A)
    nw = norm_w.astype(F32).reshape(1, D_INNER_A)
    ya = pl.pallas_call(
        _dec_norm_kernel, grid=(1,),
        in_specs=[full(y.shape), full(xs.shape), full(z.shape), full(dexp.shape), full(nw.shape)],
        out_specs=full(y.shape), out_shape=jax.ShapeDtypeStruct(y.shape, F32),
        compiler_params=_params(1), name="ssd_decode_norm")(y, xs, z, dexp, nw)
    conv_new = jnp.concatenate([conv0[:, 1:], xbc[:, None, :]], axis=1)
    return ya, hn, conv_new


def _flash_first(st, h, s_t, v_t):
    m_s, l_s, acc_s = st
    m = jnp.max(s_t, axis=0, keepdims=True)
    p = jnp.exp(s_t - m)
    m_s[h] = m
    l_s[h] = jnp.sum(p, axis=0, keepdims=True)
    acc_s[h] = jnp.dot(v_t, p.astype(BF16), preferred_element_type=F32)


def _flash_next(st, h, s_t, v_t):
    m_s, l_s, acc_s = st
    m = m_s[h]
    m_new = jnp.maximum(m, jnp.max(s_t, axis=0, keepdims=True))
    alpha = jnp.exp(m - m_new)
    p = jnp.exp(s_t - m_new)
    m_s[h] = m_new
    l_s[h] = alpha * l_s[h] + jnp.sum(p, axis=0, keepdims=True)
    acc_s[h] = alpha * acc_s[h] + jnp.dot(v_t, p.astype(BF16), preferred_element_type=F32)


def _stage_kv(nb, h, width, k_ref, v_ref, kb, v_t):
    t = ATTN_TILE
    means = []
    for j in range(nb):
        kj = k_ref[j * t:(j + 1) * t, h * width:(h + 1) * width]
        kb[h, j * t:(j + 1) * t, :] = kj.astype(BF16)
        v_t[h, j] = v_ref[j * t:(j + 1) * t, h * width:(h + 1) * width].T.astype(BF16)
        means.append(jnp.mean(kj, axis=0, keepdims=True))
    return means


def _kv_chunk(kb, v_t, h, j0, ch):
    t = ATTN_TILE
    k = kb[h, pl.ds(pl.multiple_of(j0 * t, t), ch * t), :]
    v = jnp.concatenate([v_t[h, j0 + b] for b in range(ch)], axis=1) if ch > 1 else v_t[h, j0]
    return k, v


def _past_block_loops(qi, ch, step):
    n_full = qi // ch

    def chunks(c, carry):
        step(c * ch, ch)
        return carry

    def singles(j, carry):
        step(j, 1)
        return carry

    lax.fori_loop(0, n_full, chunks, 0)
    lax.fori_loop(n_full * ch, qi, singles, 0)


def _flash_scratch(nq, width, n_queries):
    t = ATTN_TILE
    return [pltpu.VMEM((ATTN_HEADS, nq * t, width), BF16), pltpu.VMEM((ATTN_HEADS, nq, width, t), BF16),
            pltpu.VMEM((ATTN_HEADS, 1, n_queries), F32), pltpu.VMEM((ATTN_HEADS, 1, n_queries), F32),
            pltpu.VMEM((ATTN_HEADS, width, n_queries), F32)]


def _moba_prompt_kernel(nb, q_ref, k_ref, v_ref, g_ref, o_ref, kb, v_t, m_s, l_s, acc_s, kmean, qs_s, bias_s):
    t = ATTN_TILE
    qi = pl.program_id(2)
    scale = HD_B ** -0.5
    st = (m_s, l_s, acc_s)

    @pl.when(qi == 0)
    def _():
        for h in range(ATTN_HEADS):
            kmean[h] = jnp.concatenate(_stage_kv(nb, h, HD_B, k_ref, v_ref, kb, v_t), axis=0)

    rowb = lax.broadcasted_iota(jnp.int32, (nb, t), 0)
    rowf = rowb.astype(F32)
    ri = lax.broadcasted_iota(jnp.int32, (t, t), 0)
    ci = lax.broadcasted_iota(jnp.int32, (t, t), 1)
    for h in range(ATTN_HEADS):
        q = q_ref[:, h * HD_B:(h + 1) * HD_B]
        qs = (q * scale).astype(BF16)
        gate = jnp.where(rowb < qi, _mm_nt_hp(kmean[h], q), NEG_INF)
        sel = jnp.zeros((nb, t), F32)
        for _ in range(min(MOBA_TOPK, nb)):
            mx = jnp.max(gate, axis=0, keepdims=True)
            first = jnp.min(jnp.where(gate == mx, rowf, float(nb)), axis=0, keepdims=True)
            hit = rowf == first
            sel = jnp.maximum(sel, jnp.where(hit, jnp.where(mx > NEG_INF, 1.0, 0.0), 0.0))
            gate = jnp.where(hit, NEG_INF, gate)
        bias_s[h] = jnp.where(sel > 0.0, 0.0, MASK_BIAS)[:, None, :]
        qs_s[h] = qs
        k_own, v_own = _kv_chunk(kb, v_t, h, qi, 1)
        _flash_first(st, h, jnp.where(ri <= ci, _mm_nt(k_own, qs), NEG_INF), v_own)

    def step(j0, ch):
        for h in range(ATTN_HEADS):
            k, v = _kv_chunk(kb, v_t, h, j0, ch)
            s = _mm_nt(k, qs_s[h]).reshape(ch, t, t) + bias_s[h, pl.ds(j0, ch)]
            _flash_next(st, h, s.reshape(ch * t, t), v)

    _past_block_loops(qi, ATTN_CHUNK, step)
    for h in range(ATTN_HEADS):
        sl = slice(h * HD_B, (h + 1) * HD_B)
        o_ref[:, sl] = (acc_s[h] / l_s[h]).T * _silu(g_ref[:, sl])


def _moba_prompt(q, k, v, g, batch, length):
    t = ATTN_TILE
    nq = length // t
    assert H_B % ATTN_HEADS == 0
    w = ATTN_HEADS * HD_B
    blk = pl.BlockSpec((t, w), lambda b, h, i: (b * nq + i, h))
    seq = pl.BlockSpec((length, w), lambda b, h, i: (b, h))
    return pl.pallas_call(
        functools.partial(_moba_prompt_kernel, nq), grid=(batch, H_B // ATTN_HEADS, nq),
        in_specs=[blk, seq, seq, blk], out_specs=blk,
        out_shape=jax.ShapeDtypeStruct((batch * length, D_INNER_B), F32),
        scratch_shapes=_flash_scratch(nq, HD_B, t) + [pltpu.VMEM((ATTN_HEADS, nq, HD_B), F32),
                                                      pltpu.VMEM((ATTN_HEADS, t, HD_B), BF16),
                                                      pltpu.VMEM((ATTN_HEADS, nq, 1, t), F32)],
        compiler_params=_params(3), name="moba_prompt")(q, k, v, g)


def _lambda(lq1, lk1, lq2, lk2, lam_init):
    return (jnp.exp(jnp.sum(lq1[...] * lk1[...], axis=-1, keepdims=True))
            - jnp.exp(jnp.sum(lq2[...] * lk2[...], axis=-1, keepdims=True)) + lam_init)


def _diff_prompt_kernel(nb, lam_init, q_ref, k_ref, v_ref, g_ref, lq1, lk1, lq2, lk2, sw_ref, o_ref,
                        kb, v_t, m_s, l_s, acc_s, q2_s):
    t = ATTN_TILE
    qi = pl.program_id(2)
    scale = DK_C ** -0.5
    st = (m_s, l_s, acc_s)

    @pl.when(qi == 0)
    def _():
        for h in range(ATTN_HEADS):
            _stage_kv(nb, h, DV_C, k_ref, v_ref, kb, v_t)

    lam = _lambda(lq1, lk1, lq2, lk2, lam_init)
    lane = lax.broadcasted_iota(jnp.int32, (t, DV_C), 1)
    ri = lax.broadcasted_iota(jnp.int32, (t, 2 * t), 0)
    ci = lax.broadcasted_iota(jnp.int32, (t, 2 * t), 1)
    causal = ri <= jnp.where(ci >= t, ci - t, ci)
    for h in range(ATTN_HEADS):
        qs = q_ref[:, h * DV_C:(h + 1) * DV_C] * scale
        q2 = jnp.concatenate([jnp.where(lane < DK_C, qs, 0.0), jnp.where(lane >= DK_C, qs, 0.0)],
                             axis=0).astype(BF16)
        q2_s[h] = q2
        k_own, v_own = _kv_chunk(kb, v_t, h, qi, 1)
        _flash_first(st, h, jnp.where(causal, _mm_nt(k_own, q2), NEG_INF), v_own)

    def step(j0, ch):
        for h in range(ATTN_HEADS):
            k, v = _kv_chunk(kb, v_t, h, j0, ch)
            _flash_next(st, h, _mm_nt(k, q2_s[h]), v)

    _past_block_loops(qi, ATTN_CHUNK, step)
    for h in range(ATTN_HEADS):
        sl = slice(h * DV_C, (h + 1) * DV_C)
        o_t = acc_s[h] / l_s[h]
        o = (o_t[:, :t] - lam * o_t[:, t:]).T
        o = o * lax.rsqrt(jnp.mean(o * o, axis=-1, keepdims=True) + EPS) * sw_ref[...] * (1.0 - lam_init)
        o_ref[:, sl] = o * _silu(g_ref[:, sl])


def _diff_prompt(q, k, v, g, lams, subln_w, lam_init, batch, length):
    t = ATTN_TILE
    nq = length // t
    assert H_C % ATTN_HEADS == 0
    w = ATTN_HEADS * DV_C
    blk = pl.BlockSpec((t, w), lambda b, h, i: (b * nq + i, h))
    seq = pl.BlockSpec((length, w), lambda b, h, i: (b, h))
    consts = [x.astype(F32).reshape(1, DK_C) for x in lams] + [subln_w.astype(F32).reshape(1, DV_C)]
    return pl.pallas_call(
        functools.partial(_diff_prompt_kernel, nq, lam_init), grid=(batch, H_C // ATTN_HEADS, nq),
        in_specs=[blk, seq, seq, blk] + [_const_spec(c.shape) for c in consts], out_specs=blk,
        out_shape=jax.ShapeDtypeStruct((batch * length, D_INNER_C), F32),
        scratch_shapes=_flash_scratch(nq, DV_C, 2 * t) + [pltpu.VMEM((ATTN_HEADS, 2 * t, DV_C), BF16)],
        compiler_params=_params(3), name="diff_prompt")(q, k, v, g, *consts)


def _lane_sum_matrix(seg):
    r = lax.broadcasted_iota(jnp.int32, (LANES, LANES), 0)
    c = lax.broadcasted_iota(jnp.int32, (LANES, LANES), 1)
    return jnp.where(r // seg == c // seg, 1.0, 0.0).astype(F32)


def _swap_halves(a):
    return pltpu.roll(a, LANES // 2, a.ndim - 1)


def _page_partial(qs, k_ref, v_ref, jmat, two_maps, want_ksum):
    kk = k_ref[...]
    vv = v_ref[...]
    rows = kk.shape[0] * kk.shape[1]
    s = _mm_sel_rhs((kk * qs[None]).reshape(rows, LANES), jmat, parts=2).reshape(kk.shape)
    m = jnp.max(s, axis=0)
    p = jnp.exp(s - m[None])
    l = jnp.sum(p, axis=0)
    accs = [jnp.sum(p * vv, axis=0)]
    if two_maps:
        p_sw = _swap_halves(p.reshape(rows, LANES)).reshape(kk.shape)
        accs.append(jnp.sum(p_sw * vv, axis=0))
    return m, l, accs, (jnp.sum(kk, axis=0) if want_ksum else None)


def _moba_decode_kernel(n_pages, pt_ref, q_ref, kn_ref, vn_ref, g_ref, *refs):
    k_refs = refs[:n_pages]
    v_refs = refs[n_pages:2 * n_pages]
    o_ref = refs[2 * n_pages]
    scale = HD_B ** -0.5
    ppb = MOBA_BLOCK // PAGE_SIZE
    nb = n_pages // ppb
    ones = _lane_sum_matrix(LANES)
    q = q_ref[0]
    qs = q * scale
    parts = [_page_partial(qs, k_refs[t], v_refs[t], ones, False, True) for t in range(n_pages)]
    kq = [sum(pt[3] for pt in parts[b * ppb:(b + 1) * ppb]) * (1.0 / MOBA_BLOCK) * q for b in range(nb)]
    gates = _mm_sel_rhs(jnp.concatenate(kq, axis=0), ones)
    gate = [gates[b * H_B:(b + 1) * H_B] for b in range(nb)]
    chosen = []
    for b in range(nb):
        rank = jnp.zeros((H_B, LANES), F32)
        for b2 in range(nb):
            if b2 != b:
                beats = (gate[b2] >= gate[b]) if b2 < b else (gate[b2] > gate[b])
                rank = rank + jnp.where(beats, 1.0, 0.0)
        chosen.append(rank < float(MOBA_TOPK))
    s_self = _mm_sel_rhs(qs * kn_ref[0], ones)
    mtot = s_self
    for t, (m, _, _, _) in enumerate(parts):
        mtot = jnp.maximum(mtot, jnp.where(chosen[t // ppb], m, NEG_INF))
    l_tot = jnp.exp(s_self - mtot)
    acc = l_tot * vn_ref[0]
    for t, (m, l, a, _) in enumerate(parts):
        w = jnp.where(chosen[t // ppb], jnp.exp(jnp.minimum(m - mtot, 0.0)), 0.0)
        l_tot = l_tot + w * l
        acc = acc + w * a[0]
    o_ref[0] = (acc / l_tot) * _silu(g_ref[0])


def _diff_decode_kernel(n_pages, lam_init, pt_ref, q_ref, kn_ref, vn_ref, g_ref,
                        lq1, lk1, lq2, lk2, sw_ref, *refs):
    k_refs = refs[:n_pages]
    v_refs = refs[n_pages:2 * n_pages]
    o_ref = refs[2 * n_pages]
    scale = DK_C ** -0.5
    lam = _lambda(lq1, lk1, lq2, lk2, lam_init)
    jmat = _lane_sum_matrix(DK_C)
    qs = q_ref[0] * scale
    parts = [_page_partial(qs, k_refs[t], v_refs[t], jmat, True, False) for t in range(n_pages)]
    s_self = _mm_sel_rhs(qs * kn_ref[0], jmat)
    mtot = s_self
    for m, _, _, _ in parts:
        mtot = jnp.maximum(mtot, m)
    p_self = jnp.exp(s_self - mtot)
    vn = vn_ref[0]
    l_tot = p_self
    acc_a = p_self * vn
    acc_b = _swap_halves(p_self) * vn
    for m, l, a, _ in parts:
        w = jnp.exp(m - mtot)
        l_tot = l_tot + w * l
        acc_a = acc_a + w * a[0]
        acc_b = acc_b + _swap_halves(w) * a[1]
    first = lax.broadcasted_iota(jnp.int32, (H_C, LANES), 1) < DK_C
    l_sw = _swap_halves(l_tot)
    o1 = jnp.where(first, acc_a, acc_b) / jnp.where(first, l_tot, l_sw)
    o2 = jnp.where(first, acc_b, acc_a) / jnp.where(first, l_sw, l_tot)
    o = o1 - lam * o2
    o = o * lax.rsqrt(jnp.mean(o * o, axis=-1, keepdims=True) + EPS) * sw_ref[...] * (1.0 - lam_init)
    o_ref[0] = o * _silu(g_ref[0])


def _decode_attn(kernel_fn, q, kn, vn, g, consts, cache_k, cache_v, li, pt_flat, n_pages, name):
    s = q.shape[0]
    heads, hd = cache_k.shape[3], cache_k.shape[4]
    r3 = lambda a: a.reshape(s, heads, hd)
    row = pl.BlockSpec((1, heads, hd), lambda i, pt: (i, 0, 0))

    def page(t):
        return pl.BlockSpec((None, None, PAGE_SIZE, heads, hd),
                            lambda i, pt: (li, pt[i * n_pages + t], 0, 0, 0))

    const_specs = [pl.BlockSpec(c.shape, lambda i, pt, nd=c.ndim: (0,) * nd) for c in consts]
    grid_spec = pltpu.PrefetchScalarGridSpec(
        num_scalar_prefetch=1, grid=(s,),
        in_specs=[row, row, row, row] + const_specs + [page(t) for t in range(n_pages)] * 2,
        out_specs=row)
    out = pl.pallas_call(
        kernel_fn, grid_spec=grid_spec, out_shape=jax.ShapeDtypeStruct((s, heads, hd), F32),
        compiler_params=_params(1), name=name)(
            pt_flat, r3(q), r3(kn), r3(vn), r3(g), *consts, *([cache_k] * n_pages), *([cache_v] * n_pages))
    return out.reshape(s, heads * hd)


def _rope_tables(pos, rot_dim, period):
    half = rot_dim // 2
    n = pos.shape[0]
    inv = ROPE_THETA ** (-jnp.arange(half, dtype=F32) * (2.0 / rot_dim))
    ang = pos.astype(F32)[:, None] * inv[None, :]
    cos, sin = jnp.cos(ang), jnp.sin(ang)
    c = jnp.concatenate([cos, cos, jnp.ones((n, period - rot_dim), F32)], axis=1)
    s1 = jnp.concatenate([-sin, jnp.zeros((n, period - half), F32)], axis=1)
    s2 = jnp.concatenate([jnp.zeros((n, half), F32), sin, jnp.zeros((n, period - rot_dim), F32)], axis=1)
    reps = LANES // period
    return tuple(jnp.tile(a, (1, reps)) for a in (c, s1, s2))


def kernel(x_prompt, x_sample, cache_moba_k, cache_moba_v, cache_diff_k, cache_diff_v, state_ssm, state_conv, page_table, p_prompt, p_sample, w_in_even, conv_w, conv_b, dt_bias, a_log, d_skip, ssm_norm_w, w_out_even, w_in_odd, lambda_q1, lambda_k1, lambda_q2, lambda_k2, subln_w, w_out_odd, ln_g, ln_b, w_ple_gate, w_ple_proj):
    bp, length, _ = x_prompt.shape
    s = x_sample.shape[0]
    assert x_sample.shape[1] == 1
    n_pages = page_table.shape[1]
    past_len = n_pages * PAGE_SIZE
    assert past_len % MOBA_BLOCK == 0 and length % ATTN_TILE == 0 and length % SSD_CHUNK == 0
    assert s % 8 == 0
    pt_flat = page_table.reshape(-1).astype(jnp.int32)
    tm_p = 256
    xp = x_prompt.reshape(bp * length, D_MODEL)
    xs = x_sample.reshape(s, D_MODEL)
    pos_p = jnp.arange(length, dtype=jnp.int32)
    pos_s = jnp.full((s,), past_len, jnp.int32)
    tabs = {("b", "p"): _rope_tables(pos_p, ROT_B, HD_B), ("b", "s"): _rope_tables(pos_s, ROT_B, HD_B),
            ("c", "p"): _rope_tables(pos_p, ROT_C, DK_C), ("c", "s"): _rope_tables(pos_s, ROT_C, DK_C)}

    pm_k, pm_v, pd_k, pd_v, p_ssm, p_conv = [], [], [], [], [], []
    sm_k, sm_v, sd_k, sd_v, s_ssm, s_conv = [], [], [], [], [], []
    for i in range(DEPTH):
        li = i // 2
        post = functools.partial(_post, ln_g=ln_g[i], ln_b=ln_b[i], w_gate=w_ple_gate[i], w_proj=w_ple_proj[i])
        pp = p_prompt[i].reshape(bp * length, PLE_DIM)
        ps = p_sample[i].reshape(s, PLE_DIM)
        if i % 2 == 0:
            ssm_w = (conv_w[li], conv_b[li], a_log[li], d_skip[li], ssm_norm_w[li])
            w_outs = [w_out_even[li][:D_INNER_A], w_out_even[li][D_INNER_A:]]
            z, xbc, dt, dtT, q, k, v, g = _inproj_even(xp, tabs[("b", "p")], w_in_even[li], dt_bias[li], tm_p)
            ya, st = _ssd_prompt(z, xbc, dt, dtT, *ssm_w, bp, length)
            yb = _moba_prompt(q, k, v, g, bp, length)
            xp = post([ya, yb], w_outs, xp, pp, tm=tm_p)
            pm_k.append(k.reshape(bp, length, H_B, HD_B)); pm_v.append(v.reshape(bp, length, H_B, HD_B))
            p_ssm.append(st)
            p_conv.append(xbc.reshape(bp, length, CONV_DIM)[:, length - (CONV_W - 1):])
            z, xbc, dt, dtT, q, k, v, g = _inproj_even(xs, tabs[("b", "s")], w_in_even[li], dt_bias[li], s)
            ya, st, cv = _ssd_decode(z, xbc, dt, state_ssm[li], state_conv[li], *ssm_w)
            yb = _decode_attn(functools.partial(_moba_decode_kernel, n_pages), q, k, v, g, [],
                              cache_moba_k, cache_moba_v, li, pt_flat, n_pages, "moba_decode")
            xs = post([ya, yb], w_outs, xs, ps, tm=s)
            sm_k.append(k.reshape(s, 1, H_B, HD_B)); sm_v.append(v.reshape(s, 1, H_B, HD_B))
            s_ssm.append(st); s_conv.append(cv)
        else:
            lam_init = 0.8 - 0.6 * math.exp(-0.3 * i)
            lams = (lambda_q1[li], lambda_k1[li], lambda_q2[li], lambda_k2[li])
            q, k, v, g = _inproj_odd(xp, tabs[("c", "p")], w_in_odd[li], tm_p)
            y = _diff_prompt(q, k, v, g, lams, subln_w[li], lam_init, bp, length)
            xp = post([y], [w_out_odd[li]], xp, pp, tm=tm_p)
            pd_k.append(k.reshape(bp, length, H_C, 2 * DK_C)); pd_v.append(v.reshape(bp, length, H_C, DV_C))
            q, k, v, g = _inproj_odd(xs, tabs[("c", "s")], w_in_odd[li], s)
            consts = [x.astype(F32).reshape(1, DK_C) for x in lams] + \
                     [subln_w[li].astype(F32).reshape(1, DV_C)]
            y = _decode_attn(functools.partial(_diff_decode_kernel, n_pages, lam_init), q, k, v, g, consts,
                             cache_diff_k, cache_diff_v, li, pt_flat, n_pages, "diff_decode")
            xs = post([y], [w_out_odd[li]], xs, ps, tm=s)
            sd_k.append(k.reshape(s, 1, H_C, 2 * DK_C)); sd_v.append(v.reshape(s, 1, H_C, DV_C))
    return (xp.reshape(bp, length, D_MODEL), xs.reshape(s, 1, D_MODEL),
            jnp.stack(pm_k), jnp.stack(pm_v), jnp.stack(pd_k), jnp.stack(pd_v),
            jnp.stack(p_ssm), jnp.stack(p_conv),
            jnp.stack(sm_k), jnp.stack(sm_v), jnp.stack(sd_k), jnp.stack(sd_v),
            jnp.stack(s_ssm), jnp.stack(s_conv))
```

```python
import functools
import math

import jax
import jax.numpy as jnp
from jax import lax
from jax.experimental import pallas as pl
from jax.experimental.pallas import tpu as pltpu

F32 = jnp.float32
BF16 = jnp.bfloat16

D_MODEL = 1024
DEPTH = 4
PAGE_SIZE = 128
PLE_DIM = 256
EPS = 1e-5
ROPE_THETA = 500000.0
ALPHA = (2 * DEPTH) ** 0.25

D_INNER_A = D_MODEL
P_A = 64
H_A = D_INNER_A // P_A
G_A = 2
N_A = 128
CONV_W = 4
CONV_DIM = D_INNER_A + 2 * G_A * N_A
SSD_CHUNK = 128

H_B = 8
HD_B = D_MODEL // H_B
D_INNER_B = H_B * HD_B
ROT_B = HD_B // 4
MOBA_BLOCK = 256
MOBA_TOPK = 3

H_C = 8
DK_C = D_MODEL // (2 * H_C)
DV_C = 2 * DK_C
D_INNER_C = H_C * DV_C
ROT_C = DK_C // 4

LANES = 128
ATTN_TILE = 256
ATTN_HEADS = 2
ATTN_CHUNK = 4
VMEM_LIMIT = 56 * 1024 * 1024
NEG_INF = float("-inf")
MASK_BIAS = -1e30


def _split2(a):
    hi = a.astype(BF16)
    lo = (a - hi.astype(F32)).astype(BF16)
    return hi, lo


def _split3(a):
    hi = a.astype(BF16)
    r = a - hi.astype(F32)
    mid = r.astype(BF16)
    lo = (r - mid.astype(F32)).astype(BF16)
    return hi, mid, lo


def _mm(a, b):
    return jnp.dot(a.astype(BF16), b.astype(BF16), preferred_element_type=F32)


def _mm_nt(a, b):
    return lax.dot_general(a.astype(BF16), b.astype(BF16), (((1,), (1,)), ((), ())),
                           preferred_element_type=F32)


def _mm_sel_rhs(a, e, parts=3):
    eb = e.astype(BF16)
    pieces = _split3(a) if parts == 3 else _split2(a)
    out = jnp.dot(pieces[0], eb, preferred_element_type=F32)
    for p in pieces[1:]:
        out = out + jnp.dot(p, eb, preferred_element_type=F32)
    return out


def _mm_sel_lhs(e, b):
    eb = e.astype(BF16)
    pieces = _split3(b)
    out = jnp.dot(eb, pieces[0], preferred_element_type=F32)
    for p in pieces[1:]:
        out = out + jnp.dot(eb, p, preferred_element_type=F32)
    return out


def _mm_nt_hp(a, b):
    ah, al = _split2(a)
    bh, bl = _split2(b)
    dn = (((1,), (1,)), ((), ()))
    return (lax.dot_general(ah, bh, dn, preferred_element_type=F32)
            + lax.dot_general(ah, bl, dn, preferred_element_type=F32)
            + lax.dot_general(al, bh, dn, preferred_element_type=F32))


def _rows8(row):
    return jnp.broadcast_to(row, (8, row.shape[1]))


def _row_sel(row, e):
    return _mm_sel_rhs(_rows8(row), e)[0:1, :]


def _silu(x):
    return x * jax.nn.sigmoid(x)


def _softplus(x):
    return jnp.maximum(x, 0.0) + jnp.log1p(jnp.exp(-jnp.abs(x)))


def _rope128(x, c, s1, s2, r):
    return x * c + pltpu.roll(x, LANES - r, 1) * s1 + pltpu.roll(x, r, 1) * s2


def _seg_matrix(n_rows, n_cols, seg, transpose=False):
    if not transpose:
        lane = lax.broadcasted_iota(jnp.int32, (n_rows, n_cols), 0)
        sid = lax.broadcasted_iota(jnp.int32, (n_rows, n_cols), 1)
    else:
        sid = lax.broadcasted_iota(jnp.int32, (n_rows, n_cols), 0)
        lane = lax.broadcasted_iota(jnp.int32, (n_rows, n_cols), 1)
    return jnp.where(lane // seg == sid, 1.0, 0.0).astype(F32)


def _const_spec(shape):
    nd = len(shape)
    return pl.BlockSpec(shape, lambda *_: (0,) * nd, pipeline_mode=pl.Buffered(1))


def _params(n_grid):
    return pltpu.CompilerParams(dimension_semantics=("arbitrary",) * n_grid,
                                vmem_limit_bytes=VMEM_LIMIT)


_PROJ_CHUNK = 512


def _proj_into(xb, w_ref, o_ref, width, rope=None):
    for c0 in range(0, width, _PROJ_CHUNK):
        cw = min(_PROJ_CHUNK, width - c0)
        r = jnp.dot(xb, w_ref[:, c0:c0 + cw], preferred_element_type=F32)
        if rope is None:
            o_ref[:, c0:c0 + cw] = r
        else:
            c, s1, s2, shift = rope
            for j in range(cw // LANES):
                o_ref[:, c0 + j * LANES:c0 + (j + 1) * LANES] = _rope128(
                    r[:, j * LANES:(j + 1) * LANES], c, s1, s2, shift)


def _inproj_even_kernel(x_ref, c_ref, s1_ref, s2_ref, wz, wxbc, wdt, wdtT, wq, wk, wv, wg,
                        dtb, dtbT, z_o, xbc_o, dt_o, dtT_o, q_o, k_o, v_o, g_o):
    xb = x_ref[...].astype(BF16)
    rope = (c_ref[...], s1_ref[...], s2_ref[...], ROT_B // 2)
    _proj_into(xb, wz, z_o, D_INNER_A)
    _proj_into(xb, wxbc, xbc_o, CONV_DIM)
    dt_o[...] = _softplus(jnp.dot(xb, wdt[...], preferred_element_type=F32) + dtb[...])
    dtT_o[...] = _softplus(
        lax.dot_general(wdtT[...], xb, (((1,), (1,)), ((), ())), preferred_element_type=F32)
        + dtbT[...])
    _proj_into(xb, wq, q_o, D_INNER_B, rope)
    _proj_into(xb, wk, k_o, D_INNER_B, rope)
    _proj_into(xb, wv, v_o, D_INNER_B)
    _proj_into(xb, wg, g_o, D_INNER_B)


def _inproj_even(x, tables, w_in, dt_bias, tm):
    m = x.shape[0]
    nt = tables[0].shape[0] // tm
    o = [0]
    for s in (D_INNER_A, CONV_DIM, H_A, D_INNER_B, D_INNER_B, D_INNER_B, D_INNER_B):
        o.append(o[-1] + s)
    wb = w_in.astype(BF16)
    wz, wxbc, wdt, wq, wk, wv, wg = [wb[:, o[i]:o[i + 1]] for i in range(7)]
    wdtT = wdt.T
    dtb = dt_bias.astype(F32).reshape(1, H_A)
    dtbT = dt_bias.astype(F32).reshape(H_A, 1)
    row = lambda w: pl.BlockSpec((tm, w), lambda i: (i, 0))
    tab = pl.BlockSpec((tm, LANES), lambda i: (i % nt, 0))
    consts = [wz, wxbc, wdt, wdtT, wq, wk, wv, wg, dtb, dtbT]
    out_shape = [jax.ShapeDtypeStruct((m, D_INNER_A), F32), jax.ShapeDtypeStruct((m, CONV_DIM), F32),
                 jax.ShapeDtypeStruct((m, H_A), F32), jax.ShapeDtypeStruct((H_A, m), F32)] + \
                [jax.ShapeDtypeStruct((m, D_INNER_B), F32)] * 4
    out_specs = [row(D_INNER_A), row(CONV_DIM), row(H_A), pl.BlockSpec((H_A, tm), lambda i: (0, i))] + \
                [row(D_INNER_B)] * 4
    return pl.pallas_call(
        _inproj_even_kernel, grid=(m // tm,),
        in_specs=[row(D_MODEL), tab, tab, tab] + [_const_spec(c.shape) for c in consts],
        out_specs=out_specs, out_shape=out_shape, compiler_params=_params(1),
        name="inproj_even")(x, *tables, *consts)


def _inproj_odd_kernel(x_ref, c_ref, s1_ref, s2_ref, wq, wk, wv, wg, q_o, k_o, v_o, g_o):
    xb = x_ref[...].astype(BF16)
    rope = (c_ref[...], s1_ref[...], s2_ref[...], ROT_C // 2)
    _proj_into(xb, wq, q_o, D_INNER_C, rope)
    _proj_into(xb, wk, k_o, D_INNER_C, rope)
    _proj_into(xb, wv, v_o, D_INNER_C)
    _proj_into(xb, wg, g_o, D_INNER_C)


def _inproj_odd(x, tables, w_in, tm):
    m = x.shape[0]
    nt = tables[0].shape[0] // tm
    wb = w_in.astype(BF16)
    ws = [wb[:, i * D_INNER_C:(i + 1) * D_INNER_C] for i in range(4)]
    row = lambda w: pl.BlockSpec((tm, w), lambda i: (i, 0))
    tab = pl.BlockSpec((tm, LANES), lambda i: (i % nt, 0))
    return pl.pallas_call(
        _inproj_odd_kernel, grid=(m // tm,),
        in_specs=[row(D_MODEL), tab, tab, tab] + [_const_spec(w.shape) for w in ws],
        out_specs=[row(D_INNER_C)] * 4,
        out_shape=[jax.ShapeDtypeStruct((m, D_INNER_C), F32)] * 4,
        compiler_params=_params(1), name="inproj_odd")(x, *tables, *ws)


def _post_kernel(n_y, *refs):
    y_refs = refs[:n_y]
    w_refs = refs[n_y:2 * n_y]
    x_ref, p_ref, g_ref, b_ref, wg_ref, wp_ref, o_ref = refs[2 * n_y:]
    out = _mm(y_refs[0][...], w_refs[0][...])
    for y, w in zip(y_refs[1:], w_refs[1:]):
        out = out + _mm(y[...], w[...])
    h = ALPHA * x_ref[...] + out
    hc = h - jnp.mean(h, axis=-1, keepdims=True)
    var = jnp.mean(hc * hc, axis=-1, keepdims=True)
    hn = hc * lax.rsqrt(var + EPS) * g_ref[...] + b_ref[...]
    gate = jax.nn.sigmoid(_mm(hn, wg_ref[...]))
    o_ref[...] = hn + gate * _mm(p_ref[...], wp_ref[...])


def _post(ys, w_outs, x, p, ln_g, ln_b, w_gate, w_proj, tm):
    m = x.shape[0]
    n_y = len(ys)
    row = lambda w: pl.BlockSpec((tm, w), lambda i: (i, 0))
    consts = [w.astype(BF16) for w in w_outs]
    tail = [ln_g.astype(F32).reshape(1, D_MODEL), ln_b.astype(F32).reshape(1, D_MODEL),
            w_gate.astype(BF16), w_proj.astype(BF16)]
    return pl.pallas_call(
        functools.partial(_post_kernel, n_y), grid=(m // tm,),
        in_specs=[row(y.shape[1]) for y in ys] + [_const_spec(c.shape) for c in consts]
                 + [row(D_MODEL), row(PLE_DIM)] + [_const_spec(c.shape) for c in tail],
        out_specs=row(D_MODEL), out_shape=jax.ShapeDtypeStruct((m, D_MODEL), F32),
        compiler_params=_params(1), name="outproj_post")(*ys, *consts, x, p, *tail)


def _gated_group_norm(y, z, nw):
    yz = y * _silu(z)
    gw = D_INNER_A // G_A
    outs = []
    for g in range(G_A):
        seg = yz[:, g * gw:(g + 1) * gw]
        ms = jnp.mean(seg * seg, axis=-1, keepdims=True)
        outs.append(seg * lax.rsqrt(ms + EPS) * nw[:, g * gw:(g + 1) * gw])
    return outs


def _ssd_prompt_kernel(z_ref, xbc_ref, dt_ref, dtT_ref, cw_ref, cb_ref, alog_ref, alogT_ref,
                       dexp_ref, nw_ref, ya_ref, st_ref, cbuf, hT):
    q = SSD_CHUNK
    c = pl.program_id(1)

    @pl.when(c == 0)
    def _():
        hT[...] = jnp.zeros_like(hT)
        cbuf[0:8, :] = jnp.zeros((8, CONV_DIM), F32)

    @pl.when(c > 0)
    def _():
        cbuf[0:8, :] = cbuf[q:q + 8, :]

    cbuf[8:q + 8, :] = xbc_ref[...]
    acc = cbuf[8:q + 8, :] * cw_ref[CONV_W - 1:CONV_W, :] + cb_ref[...]
    for s in range(1, CONV_W):
        acc = acc + cbuf[8 - s:q + 8 - s, :] * cw_ref[CONV_W - 1 - s:CONV_W - s, :]
    xbc = _silu(acc)
    xs = xbc[:, :D_INNER_A]
    bm = xbc[:, D_INNER_A:D_INNER_A + G_A * N_A]
    cm = xbc[:, D_INNER_A + G_A * N_A:]

    dt = dt_ref[...]
    dtT = dtT_ref[...]
    a = -jnp.exp(alog_ref[...])
    aT = -jnp.exp(alogT_ref[...])
    ri = lax.broadcasted_iota(jnp.int32, (q, q), 0)
    ci = lax.broadcasted_iota(jnp.int32, (q, q), 1)
    causal = ri >= ci
    acs = _mm_sel_lhs(jnp.where(causal, 1.0, 0.0), dt * a)
    acsT = _mm_sel_rhs(dtT * aT, jnp.where(ri <= ci, 1.0, 0.0))
    acs_end = acs[q - 1:q, :]
    e_hp = _seg_matrix(H_A, D_INNER_A, P_A, transpose=True)
    w_exp = _mm_sel_rhs(jnp.exp(acs_end - acs) * dt, e_hp)
    eacs_exp = _mm_sel_rhs(jnp.exp(acs), e_hp)
    cd_exp = _row_sel(jnp.exp(acs_end), e_hp)
    xw = xs * w_exp

    lane = lax.broadcasted_iota(jnp.int32, (q, LANES), 1)
    rep = H_A // G_A
    cb_g, bT_g = [], []
    for g in range(G_A):
        b_g = bm[:, g * N_A:(g + 1) * N_A]
        c_g = cm[:, g * N_A:(g + 1) * N_A]
        cb_g.append(_mm_nt(c_g, b_g))
        bT_g.append(b_g.T)

    for k in range(H_A // 2):
        g = (2 * k) // rep
        sl = slice(k * LANES, (k + 1) * LANES)
        ws = []
        for hh in (2 * k, 2 * k + 1):
            seg = acs[:, hh:hh + 1] - acsT[hh:hh + 1, :]
            dec = jnp.exp(jnp.where(causal, seg, NEG_INF))
            ws.append((cb_g[g] * dec * dtT[hh:hh + 1, :]).astype(BF16))
        xp = xs[:, sl]
        rhs = jnp.concatenate([jnp.where(lane < P_A, xp, 0.0), jnp.where(lane >= P_A, xp, 0.0)], axis=0)
        y_diag = jnp.dot(jnp.concatenate(ws, axis=1), rhs.astype(BF16), preferred_element_type=F32)
        h_in = hT[:, sl]
        y_off = _mm(cm[:, g * N_A:(g + 1) * N_A], h_in) * eacs_exp[:, sl]
        ya_ref[:, sl] = y_diag + y_off + dexp_ref[:, sl] * xp
        hT[:, sl] = h_in * cd_exp[:, sl] + _mm(bT_g[g], xw[:, sl])

    outs = _gated_group_norm(ya_ref[...], z_ref[...], nw_ref[...])
    gw = D_INNER_A // G_A
    for g in range(G_A):
        ya_ref[:, g * gw:(g + 1) * gw] = outs[g]

    @pl.when(c == pl.num_programs(1) - 1)
    def _():
        st_ref[...] = hT[...]


def _ssd_prompt(z, xbc, dt, dtT, conv_w, conv_b, a_log, d_skip, norm_w, batch, length):
    q = SSD_CHUNK
    nc = length // q
    blk = lambda w: pl.BlockSpec((q, w), lambda b, c: (b * nc + c, 0))
    consts = [conv_w.astype(F32), conv_b.astype(F32).reshape(1, CONV_DIM),
              a_log.astype(F32).reshape(1, H_A), a_log.astype(F32).reshape(H_A, 1),
              jnp.repeat(d_skip.astype(F32), P_A).reshape(1, D_INNER_A),
              norm_w.astype(F32).reshape(1, D_INNER_A)]
    ya, st = pl.pallas_call(
        _ssd_prompt_kernel, grid=(batch, nc),
        in_specs=[blk(D_INNER_A), blk(CONV_DIM), blk(H_A),
                  pl.BlockSpec((H_A, q), lambda b, c: (0, b * nc + c))]
                 + [_const_spec(cc.shape) for cc in consts],
        out_specs=[blk(D_INNER_A), pl.BlockSpec((None, N_A, D_INNER_A), lambda b, c: (b, 0, 0))],
        out_shape=[jax.ShapeDtypeStruct((batch * length, D_INNER_A), F32),
                   jax.ShapeDtypeStruct((batch, N_A, D_INNER_A), F32)],
        scratch_shapes=[pltpu.VMEM((q + 8, CONV_DIM), F32), pltpu.VMEM((N_A, D_INNER_A), F32)],
        compiler_params=_params(2), name="ssd_prompt")(z, xbc, dt, dtT, *consts)
    state = st.reshape(batch, N_A, H_A, P_A).transpose(0, 2, 3, 1)
    return ya, state


def _dec_conv_kernel(c0_ref, xbc_ref, cw_ref, cb_ref, o_ref):
    acc = xbc_ref[...] * cw_ref[CONV_W - 1:CONV_W, :] + cb_ref[...]
    for tap in range(CONV_W - 1):
        acc = acc + c0_ref[tap] * cw_ref[tap:tap + 1, :]
    o_ref[...] = _silu(acc)


def _dec_state_kernel(h_ref, x_ref, b_ref, c_ref, dt_ref, alog_ref, hn_ref, y_ref):
    dt = dt_ref[...]
    decay = jnp.exp(dt * (-jnp.exp(alog_ref[...])))
    hn = h_ref[...] * decay + (dt * x_ref[...]) * b_ref[...]
    hn_ref[...] = hn
    y_ref[...] = jnp.sum(hn * c_ref[...], axis=-1, keepdims=True)


def _dec_norm_kernel(y_ref, xs_ref, z_ref, dexp_ref, nw_ref, o_ref):
    y = y_ref[...] + dexp_ref[...] * xs_ref[...]
    outs = _gated_group_norm(y, z_ref[...], nw_ref[...])
    gw = D_INNER_A // G_A
    for g in range(G_A):
        o_ref[:, g * gw:(g + 1) * gw] = outs[g]


def _ssd_decode(z, xbc, dt, state_all, li, conv0, conv_w, conv_b, a_log, d_skip, norm_w):
    s = z.shape[0]
    full = lambda shape: pl.BlockSpec(shape, lambda *_: (0,) * len(shape))
    c0t = conv0.transpose(1, 0, 2)
    cw = conv_w.astype(F32)
    cb = conv_b.astype(F32).reshape(1, CONV_DIM)
    act = pl.pallas_call(
        _dec_conv_kernel, grid=(1,),
        in_specs=[full(c0t.shape), full(xbc.shape), full(cw.shape), full(cb.shape)],
        out_specs=full(xbc.shape), out_shape=jax.ShapeDtypeStruct(xbc.shape, F32),
        compiler_params=_params(1), name="ssd_decode_conv")(c0t, xbc, cw, cb)
    xs = act[:, :D_INNER_A]
    rep = H_A // G_A
    x4 = xs.reshape(s, H_A, P_A, 1)
    b4 = jnp.repeat(act[:, D_INNER_A:D_INNER_A + G_A * N_A].reshape(s, G_A, 1, N_A), rep, axis=1)
    c4 = jnp.repeat(act[:, D_INNER_A + G_A * N_A:].reshape(s, G_A, 1, N_A), rep, axis=1)
    dt4 = dt.reshape(s, H_A, 1, 1)
    alog4 = a_log.astype(F32).reshape(1, H_A, 1, 1)
    ts = 8
    seq = lambda a, b: pl.BlockSpec((ts, H_A, a, b), lambda i: (i, 0, 0, 0))
    hn, y4 = pl.pallas_call(
        _dec_state_kernel, grid=(s // ts,),
        in_specs=[pl.BlockSpec((None, ts, H_A, P_A, N_A), lambda i: (li, i, 0, 0, 0)), seq(P_A, 1), seq(1, N_A), seq(1, N_A), seq(1, 1),
                  pl.BlockSpec((1, H_A, 1, 1), lambda i: (0, 0, 0, 0))],
        out_specs=[seq(P_A, N_A), seq(P_A, 1)],
        out_shape=[jax.ShapeDtypeStruct((s, H_A, P_A, N_A), F32),
                   jax.ShapeDtypeStruct((s, H_A, P_A, 1), F32)],
        compiler_params=_params(1), name="ssd_decode_state")(state_all, x4, b4, c4, dt4, alog4)
    y = y4.reshape(s, D_INNER_A)
    dexp = jnp.repeat(d_skip.astype(F32), P_A).reshape(1, D_INNER_A)
    nw = norm_w.astype(F32).reshape(1, D_INNER_A)
    ya = pl.pallas_call(
        _dec_norm_kernel, grid=(1,),
        in_specs=[full(y.shape), full(xs.shape), full(z.shape), full(dexp.shape), full(nw.shape)],
        out_specs=full(y.shape), out_shape=jax.ShapeDtypeStruct(y.shape, F32),
        compiler_params=_params(1), name="ssd_decode_norm")(y, xs, z, dexp, nw)
    conv_new = jnp.concatenate([conv0[:, 1:], xbc[:, None, :]], axis=1)
    return ya, hn, conv_new


def _flash_next(st, h, s_t, v_t):
    m_s, l_s, acc_s = st
    m = m_s[h]
    m_new = jnp.maximum(m, jnp.max(s_t, axis=0, keepdims=True))
    alpha = jnp.exp(m - m_new)
    p = jnp.exp(s_t - m_new)
    m_s[h] = m_new
    l_s[h] = alpha * l_s[h] + jnp.sum(p, axis=0, keepdims=True)
    acc_s[h] = alpha * acc_s[h] + jnp.dot(v_t, p.astype(BF16), preferred_element_type=F32)


def _stage_kv(nb, h, width, k_ref, v_ref, kb, v_t):
    t = ATTN_TILE
    means = []
    for j in range(nb):
        kj = k_ref[j * t:(j + 1) * t, h * width:(h + 1) * width]
        kb[h, j * t:(j + 1) * t, :] = kj.astype(BF16)
        v_t[h, j] = v_ref[j * t:(j + 1) * t, h * width:(h + 1) * width].T.astype(BF16)
        means.append(jnp.mean(kj, axis=0, keepdims=True))
    return means


def _chunk_keys(kb, h, j0):
    t = ATTN_TILE
    return kb[h, pl.ds(pl.multiple_of(j0 * t, t), ATTN_CHUNK * t), :]


def _causal_chunk_mask(qi, j0, n_cols):
    t = ATTN_TILE
    ri = lax.broadcasted_iota(jnp.int32, (ATTN_CHUNK * t, n_cols), 0)
    ci = lax.broadcasted_iota(jnp.int32, (ATTN_CHUNK * t, n_cols), 1)
    for rep in range(1, n_cols // t):
        ci = jnp.where(ci >= rep * t, ci - t, ci)
    return ri + (j0 - qi) * t <= ci


def _chunked_flash(qi, st, s_buf, v_t, scores):
    m_s, l_s, acc_s = st
    ch = ATTN_CHUNK
    n_full = qi // ch
    j_last = n_full * ch
    for h in range(ATTN_HEADS):
        m_s[h] = jnp.full(m_s.shape[1:], NEG_INF, F32)
        l_s[h] = jnp.zeros(l_s.shape[1:], F32)
        acc_s[h] = jnp.zeros(acc_s.shape[1:], F32)
        s_buf[h, 0] = scores(h, j_last, True)

    def consume(slot, j0):
        for h in range(ATTN_HEADS):
            v = jnp.concatenate([v_t[h, j0 + b] for b in range(ch)], axis=1)
            _flash_next(st, h, s_buf[h, slot], v)

    def body(c, j_prev):
        slot = c % 2
        consume(slot, j_prev)
        for h in range(ATTN_HEADS):
            s_buf[h, 1 - slot] = scores(h, c * ch, False)
        return c * ch

    j_prev = lax.fori_loop(0, n_full, body, j_last)
    consume(n_full % 2, j_prev)


def _flash_scratch(nq, width, n_queries):
    t = ATTN_TILE
    return [pltpu.VMEM((ATTN_HEADS, nq * t, width), BF16), pltpu.VMEM((ATTN_HEADS, nq, width, t), BF16),
            pltpu.VMEM((ATTN_HEADS, 1, n_queries), F32), pltpu.VMEM((ATTN_HEADS, 1, n_queries), F32),
            pltpu.VMEM((ATTN_HEADS, width, n_queries), F32),
            pltpu.VMEM((ATTN_HEADS, 2, ATTN_CHUNK * t, n_queries), F32)]


def _moba_prompt_kernel(nb, q_ref, k_ref, v_ref, g_ref, o_ref, kb, v_t, m_s, l_s, acc_s, s_buf,
                        kmean, qs_s, bias_s):
    t = ATTN_TILE
    qi = pl.program_id(2)
    scale = HD_B ** -0.5
    st = (m_s, l_s, acc_s)

    @pl.when(qi == 0)
    def _():
        for h in range(ATTN_HEADS):
            kmean[h] = jnp.concatenate(_stage_kv(nb, h, HD_B, k_ref, v_ref, kb, v_t), axis=0)

    rowb = lax.broadcasted_iota(jnp.int32, (nb, t), 0)
    rowf = rowb.astype(F32)
    for h in range(ATTN_HEADS):
        q = q_ref[:, h * HD_B:(h + 1) * HD_B]
        qs = (q * scale).astype(BF16)
        gate = jnp.where(rowb < qi, _mm_nt_hp(kmean[h], q), NEG_INF)
        sel = jnp.zeros((nb, t), F32)
        for _ in range(min(MOBA_TOPK, nb)):
            mx = jnp.max(gate, axis=0, keepdims=True)
            first = jnp.min(jnp.where(gate == mx, rowf, float(nb)), axis=0, keepdims=True)
            hit = rowf == first
            sel = jnp.maximum(sel, jnp.where(hit, jnp.where(mx > NEG_INF, 1.0, 0.0), 0.0))
            gate = jnp.where(hit, NEG_INF, gate)
        bias_s[h] = jnp.where(rowb == qi, 0.0, jnp.where(sel > 0.0, 0.0, MASK_BIAS))[:, None, :]
        qs_s[h] = qs

    def scores(h, j0, last):
        s = _mm_nt(_chunk_keys(kb, h, j0), qs_s[h]).reshape(ATTN_CHUNK, t, t) + bias_s[h, pl.ds(j0, ATTN_CHUNK)]
        s = s.reshape(ATTN_CHUNK * t, t)
        return jnp.where(_causal_chunk_mask(qi, j0, t), s, NEG_INF) if last else s

    _chunked_flash(qi, st, s_buf, v_t, scores)
    for h in range(ATTN_HEADS):
        sl = slice(h * HD_B, (h + 1) * HD_B)
        o_ref[:, sl] = (acc_s[h] / l_s[h]).T * _silu(g_ref[:, sl])


def _moba_prompt(q, k, v, g, batch, length):
    t = ATTN_TILE
    nq = length // t
    assert H_B % ATTN_HEADS == 0 and nq % ATTN_CHUNK == 0
    w = ATTN_HEADS * HD_B
    blk = pl.BlockSpec((t, w), lambda b, h, i: (b * nq + i, h))
    seq = pl.BlockSpec((length, w), lambda b, h, i: (b, h))
    return pl.pallas_call(
        functools.partial(_moba_prompt_kernel, nq), grid=(batch, H_B // ATTN_HEADS, nq),
        in_specs=[blk, seq, seq, blk], out_specs=blk,
        out_shape=jax.ShapeDtypeStruct((batch * length, D_INNER_B), F32),
        scratch_shapes=_flash_scratch(nq, HD_B, t) + [pltpu.VMEM((ATTN_HEADS, nq, HD_B), F32),
                                                      pltpu.VMEM((ATTN_HEADS, t, HD_B), BF16),
                                                      pltpu.VMEM((ATTN_HEADS, nq, 1, t), F32)],
        compiler_params=_params(3), name="moba_prompt")(q, k, v, g)


def _lambda(lq1, lk1, lq2, lk2, lam_init):
    return (jnp.exp(jnp.sum(lq1[...] * lk1[...], axis=-1, keepdims=True))
            - jnp.exp(jnp.sum(lq2[...] * lk2[...], axis=-1, keepdims=True)) + lam_init)


def _diff_prompt_kernel(nb, lam_init, q_ref, k_ref, v_ref, g_ref, lq1, lk1, lq2, lk2, sw_ref, o_ref,
                        kb, v_t, m_s, l_s, acc_s, s_buf, q2_s):
    t = ATTN_TILE
    qi = pl.program_id(2)
    scale = DK_C ** -0.5
    st = (m_s, l_s, acc_s)

    @pl.when(qi == 0)
    def _():
        for h in range(ATTN_HEADS):
            _stage_kv(nb, h, DV_C, k_ref, v_ref, kb, v_t)

    lam = _lambda(lq1, lk1, lq2, lk2, lam_init)
    lane = lax.broadcasted_iota(jnp.int32, (t, DV_C), 1)
    for h in range(ATTN_HEADS):
        qs = q_ref[:, h * DV_C:(h + 1) * DV_C] * scale
        q2_s[h] = jnp.concatenate([jnp.where(lane < DK_C, qs, 0.0), jnp.where(lane >= DK_C, qs, 0.0)],
                                  axis=0).astype(BF16)

    def scores(h, j0, last):
        s = _mm_nt(_chunk_keys(kb, h, j0), q2_s[h])
        return jnp.where(_causal_chunk_mask(qi, j0, 2 * t), s, NEG_INF) if last else s

    _chunked_flash(qi, st, s_buf, v_t, scores)
    for h in range(ATTN_HEADS):
        sl = slice(h * DV_C, (h + 1) * DV_C)
        o_t = acc_s[h] / l_s[h]
        o = (o_t[:, :t] - lam * o_t[:, t:]).T
        o = o * lax.rsqrt(jnp.mean(o * o, axis=-1, keepdims=True) + EPS) * sw_ref[...] * (1.0 - lam_init)
        o_ref[:, sl] = o * _silu(g_ref[:, sl])


def _diff_prompt(q, k, v, g, lams, subln_w, lam_init, batch, length):
    t = ATTN_TILE
    nq = length // t
    assert H_C % ATTN_HEADS == 0 and nq % ATTN_CHUNK == 0
    w = ATTN_HEADS * DV_C
    blk = pl.BlockSpec((t, w), lambda b, h, i: (b * nq + i, h))
    seq = pl.BlockSpec((length, w), lambda b, h, i: (b, h))
    consts = [x.astype(F32).reshape(1, DK_C) for x in lams] + [subln_w.astype(F32).reshape(1, DV_C)]
    return pl.pallas_call(
        functools.partial(_diff_prompt_kernel, nq, lam_init), grid=(batch, H_C // ATTN_HEADS, nq),
        in_specs=[blk, seq, seq, blk] + [_const_spec(c.shape) for c in consts], out_specs=blk,
        out_shape=jax.ShapeDtypeStruct((batch * length, D_INNER_C), F32),
        scratch_shapes=_flash_scratch(nq, DV_C, 2 * t) + [pltpu.VMEM((ATTN_HEADS, 2 * t, DV_C), BF16)],
        compiler_params=_params(3), name="diff_prompt")(q, k, v, g, *consts)


def _lane_sum_matrix(seg):
    r = lax.broadcasted_iota(jnp.int32, (LANES, LANES), 0)
    c = lax.broadcasted_iota(jnp.int32, (LANES, LANES), 1)
    return jnp.where(r // seg == c // seg, 1.0, 0.0).astype(F32)


def _swap_halves(a):
    return pltpu.roll(a, LANES // 2, a.ndim - 1)


def _page_partial(qs, k_ref, v_ref, jmat, two_maps, want_ksum):
    kk = k_ref[...]
    vv = v_ref[...]
    rows = kk.shape[0] * kk.shape[1]
    s = _mm_sel_rhs((kk * qs[None]).reshape(rows, LANES), jmat, parts=2).reshape(kk.shape)
    m = jnp.max(s, axis=0)
    p = jnp.exp(s - m[None])
    l = jnp.sum(p, axis=0)
    accs = [jnp.sum(p * vv, axis=0)]
    if two_maps:
        p_sw = _swap_halves(p.reshape(rows, LANES)).reshape(kk.shape)
        accs.append(jnp.sum(p_sw * vv, axis=0))
    return m, l, accs, (jnp.sum(kk, axis=0) if want_ksum else None)


def _moba_decode_kernel(n_pages, pt_ref, q_ref, kn_ref, vn_ref, g_ref, *refs):
    k_refs = refs[:n_pages]
    v_refs = refs[n_pages:2 * n_pages]
    o_ref = refs[2 * n_pages]
    scale = HD_B ** -0.5
    ppb = MOBA_BLOCK // PAGE_SIZE
    nb = n_pages // ppb
    ones = _lane_sum_matrix(LANES)
    q = q_ref[0]
    qs = q * scale
    parts = [_page_partial(qs, k_refs[t], v_refs[t], ones, False, True) for t in range(n_pages)]
    kq = [sum(pt[3] for pt in parts[b * ppb:(b + 1) * ppb]) * (1.0 / MOBA_BLOCK) * q for b in range(nb)]
    gates = _mm_sel_rhs(jnp.concatenate(kq, axis=0), ones)
    gate = [gates[b * H_B:(b + 1) * H_B] for b in range(nb)]
    chosen = []
    for b in range(nb):
        rank = jnp.zeros((H_B, LANES), F32)
        for b2 in range(nb):
            if b2 != b:
                beats = (gate[b2] >= gate[b]) if b2 < b else (gate[b2] > gate[b])
                rank = rank + jnp.where(beats, 1.0, 0.0)
        chosen.append(rank < float(MOBA_TOPK))
    s_self = _mm_sel_rhs(qs * kn_ref[0], ones)
    mtot = s_self
    for t, (m, _, _, _) in enumerate(parts):
        mtot = jnp.maximum(mtot, jnp.where(chosen[t // ppb], m, NEG_INF))
    l_tot = jnp.exp(s_self - mtot)
    acc = l_tot * vn_ref[0]
    for t, (m, l, a, _) in enumerate(parts):
        w = jnp.where(chosen[t // ppb], jnp.exp(jnp.minimum(m - mtot, 0.0)), 0.0)
        l_tot = l_tot + w * l
        acc = acc + w * a[0]
    o_ref[0] = (acc / l_tot) * _silu(g_ref[0])


def _diff_decode_kernel(n_pages, lam_init, pt_ref, q_ref, kn_ref, vn_ref, g_ref,
                        lq1, lk1, lq2, lk2, sw_ref, *refs):
    k_refs = refs[:n_pages]
    v_refs = refs[n_pages:2 * n_pages]
    o_ref = refs[2 * n_pages]
    scale = DK_C ** -0.5
    lam = _lambda(lq1, lk1, lq2, lk2, lam_init)
    jmat = _lane_sum_matrix(DK_C)
    qs = q_ref[0] * scale
    parts = [_page_partial(qs, k_refs[t], v_refs[t], jmat, True, False) for t in range(n_pages)]
    s_self = _mm_sel_rhs(qs * kn_ref[0], jmat)
    mtot = s_self
    for m, _, _, _ in parts:
        mtot = jnp.maximum(mtot, m)
    p_self = jnp.exp(s_self - mtot)
    vn = vn_ref[0]
    l_tot = p_self
    acc_a = p_self * vn
    acc_b = _swap_halves(p_self) * vn
    for m, l, a, _ in parts:
        w = jnp.exp(m - mtot)
        l_tot = l_tot + w * l
        acc_a = acc_a + w * a[0]
        acc_b = acc_b + _swap_halves(w) * a[1]
    first = lax.broadcasted_iota(jnp.int32, (H_C, LANES), 1) < DK_C
    l_sw = _swap_halves(l_tot)
    o1 = jnp.where(first, acc_a, acc_b) / jnp.where(first, l_tot, l_sw)
    o2 = jnp.where(first, acc_b, acc_a) / jnp.where(first, l_sw, l_tot)
    o = o1 - lam * o2
    o = o * lax.rsqrt(jnp.mean(o * o, axis=-1, keepdims=True) + EPS) * sw_ref[...] * (1.0 - lam_init)
    o_ref[0] = o * _silu(g_ref[0])


def _decode_attn(kernel_fn, q, kn, vn, g, consts, cache_k, cache_v, li, pt_flat, n_pages, name):
    s = q.shape[0]
    heads, hd = cache_k.shape[3], cache_k.shape[4]
    r3 = lambda a: a.reshape(s, heads, hd)
    row = pl.BlockSpec((1, heads, hd), lambda i, pt: (i, 0, 0))

    def page(t):
        return pl.BlockSpec((None, None, PAGE_SIZE, heads, hd),
                            lambda i, pt: (li, pt[i * n_pages + t], 0, 0, 0))

    const_specs = [pl.BlockSpec(c.shape, lambda i, pt, nd=c.ndim: (0,) * nd) for c in consts]
    grid_spec = pltpu.PrefetchScalarGridSpec(
        num_scalar_prefetch=1, grid=(s,),
        in_specs=[row, row, row, row] + const_specs + [page(t) for t in range(n_pages)] * 2,
        out_specs=row)
    out = pl.pallas_call(
        kernel_fn, grid_spec=grid_spec, out_shape=jax.ShapeDtypeStruct((s, heads, hd), F32),
        compiler_params=_params(1), name=name)(
            pt_flat, r3(q), r3(kn), r3(vn), r3(g), *consts, *([cache_k] * n_pages), *([cache_v] * n_pages))
    return out.reshape(s, heads * hd)


def _rope_tables(pos, rot_dim, period):
    half = rot_dim // 2
    n = pos.shape[0]
    inv = ROPE_THETA ** (-jnp.arange(half, dtype=F32) * (2.0 / rot_dim))
    ang = pos.astype(F32)[:, None] * inv[None, :]
    cos, sin = jnp.cos(ang), jnp.sin(ang)
    c = jnp.concatenate([cos, cos, jnp.ones((n, period - rot_dim), F32)], axis=1)
    s1 = jnp.concatenate([-sin, jnp.zeros((n, period - half), F32)], axis=1)
    s2 = jnp.concatenate([jnp.zeros((n, half), F32), sin, jnp.zeros((n, period - rot_dim), F32)], axis=1)
    reps = LANES // period
    return tuple(jnp.tile(a, (1, reps)) for a in (c, s1, s2))


def kernel(x_prompt, x_sample, cache_moba_k, cache_moba_v, cache_diff_k, cache_diff_v, state_ssm, state_conv, page_table, p_prompt, p_sample, w_in_even, conv_w, conv_b, dt_bias, a_log, d_skip, ssm_norm_w, w_out_even, w_in_odd, lambda_q1, lambda_k1, lambda_q2, lambda_k2, subln_w, w_out_odd, ln_g, ln_b, w_ple_gate, w_ple_proj):
    bp, length, _ = x_prompt.shape
    s = x_sample.shape[0]
    assert x_sample.shape[1] == 1
    n_pages = page_table.shape[1]
    past_len = n_pages * PAGE_SIZE
    assert past_len % MOBA_BLOCK == 0 and length % ATTN_TILE == 0 and length % SSD_CHUNK == 0
    assert s % 8 == 0
    pt_flat = page_table.reshape(-1).astype(jnp.int32)
    tm_p = 256
    xp = x_prompt.reshape(bp * length, D_MODEL)
    xs = x_sample.reshape(s, D_MODEL)
    pos_p = jnp.arange(length, dtype=jnp.int32)
    pos_s = jnp.full((s,), past_len, jnp.int32)
    tabs = {("b", "p"): _rope_tables(pos_p, ROT_B, HD_B), ("b", "s"): _rope_tables(pos_s, ROT_B, HD_B),
            ("c", "p"): _rope_tables(pos_p, ROT_C, DK_C), ("c", "s"): _rope_tables(pos_s, ROT_C, DK_C)}

    pm_k, pm_v, pd_k, pd_v, p_ssm, p_conv = [], [], [], [], [], []
    sm_k, sm_v, sd_k, sd_v, s_ssm, s_conv = [], [], [], [], [], []
    for i in range(DEPTH):
        li = i // 2
        post = functools.partial(_post, ln_g=ln_g[i], ln_b=ln_b[i], w_gate=w_ple_gate[i], w_proj=w_ple_proj[i])
        pp = p_prompt[i].reshape(bp * length, PLE_DIM)
        ps = p_sample[i].reshape(s, PLE_DIM)
        if i % 2 == 0:
            ssm_w = (conv_w[li], conv_b[li], a_log[li], d_skip[li], ssm_norm_w[li])
            w_outs = [w_out_even[li][:D_INNER_A], w_out_even[li][D_INNER_A:]]
            z, xbc, dt, dtT, q, k, v, g = _inproj_even(xp, tabs[("b", "p")], w_in_even[li], dt_bias[li], tm_p)
            ya, st = _ssd_prompt(z, xbc, dt, dtT, *ssm_w, bp, length)
            yb = _moba_prompt(q, k, v, g, bp, length)
            xp = post([ya, yb], w_outs, xp, pp, tm=tm_p)
            pm_k.append(k.reshape(bp, length, H_B, HD_B)); pm_v.append(v.reshape(bp, length, H_B, HD_B))
            p_ssm.append(st)
            p_conv.append(xbc.reshape(bp, length, CONV_DIM)[:, length - (CONV_W - 1):])
            z, xbc, dt, dtT, q, k, v, g = _inproj_even(xs, tabs[("b", "s")], w_in_even[li], dt_bias[li], s)
            ya, st, cv = _ssd_decode(z, xbc, dt, state_ssm, li, state_conv[li], *ssm_w)
            yb = _decode_attn(functools.partial(_moba_decode_kernel, n_pages), q, k, v, g, [],
                              cache_moba_k, cache_moba_v, li, pt_flat, n_pages, "moba_decode")
            xs = post([ya, yb], w_outs, xs, ps, tm=s)
            sm_k.append(k.reshape(s, 1, H_B, HD_B)); sm_v.append(v.reshape(s, 1, H_B, HD_B))
            s_ssm.append(st); s_conv.append(cv)
        else:
            lam_init = 0.8 - 0.6 * math.exp(-0.3 * i)
            lams = (lambda_q1[li], lambda_k1[li], lambda_q2[li], lambda_k2[li])
            q, k, v, g = _inproj_odd(xp, tabs[("c", "p")], w_in_odd[li], tm_p)
            y = _diff_prompt(q, k, v, g, lams, subln_w[li], lam_init, bp, length)
            xp = post([y], [w_out_odd[li]], xp, pp, tm=tm_p)
            pd_k.append(k.reshape(bp, length, H_C, 2 * DK_C)); pd_v.append(v.reshape(bp, length, H_C, DV_C))
            q, k, v, g = _inproj_odd(xs, tabs[("c", "s")], w_in_odd[li], s)
            consts = [x.astype(F32).reshape(1, DK_C) for x in lams] + \
                     [subln_w[li].astype(F32).reshape(1, DV_C)]
            y = _decode_attn(functools.partial(_diff_decode_kernel, n_pages, lam_init), q, k, v, g, consts,
                             cache_diff_k, cache_diff_v, li, pt_flat, n_pages, "diff_decode")
            xs = post([y], [w_out_odd[li]], xs, ps, tm=s)
            sd_k.append(k.reshape(s, 1, H_C, 2 * DK_C)); sd_v.append(v.reshape(s, 1, H_C, DV_C))
    return (xp.reshape(bp, length, D_MODEL), xs.reshape(s, 1, D_MODEL),
            jnp.stack(pm_k), jnp.stack(pm_v), jnp.stack(pd_k), jnp.stack(pd_v),
            jnp.stack(p_ssm), jnp.stack(p_conv),
            jnp.stack(sm_k), jnp.stack(sm_v), jnp.stack(sd_k), jnp.stack(sd_v),
            jnp.stack(s_ssm), jnp.stack(s_conv))
```

```python
import functools
import math

import jax
import jax.numpy as jnp
from jax import lax
from jax.experimental import pallas as pl
from jax.experimental.pallas import tpu as pltpu

F32 = jnp.float32
BF16 = jnp.bfloat16

D_MODEL = 1024
DEPTH = 4
PAGE_SIZE = 128
PLE_DIM = 256
EPS = 1e-5
ROPE_THETA = 500000.0
ALPHA = (2 * DEPTH) ** 0.25

D_INNER_A = D_MODEL
P_A = 64
H_A = D_INNER_A // P_A
G_A = 2
N_A = 128
CONV_W = 4
CONV_DIM = D_INNER_A + 2 * G_A * N_A
SSD_CHUNK = 128

H_B = 8
HD_B = D_MODEL // H_B
D_INNER_B = H_B * HD_B
ROT_B = HD_B // 4
MOBA_BLOCK = 256
MOBA_TOPK = 3

H_C = 8
DK_C = D_MODEL // (2 * H_C)
DV_C = 2 * DK_C
D_INNER_C = H_C * DV_C
ROT_C = DK_C // 4

LANES = 128
ATTN_TILE = 256
ATTN_HEADS = 2
ATTN_CHUNK = 4
VMEM_LIMIT = 56 * 1024 * 1024
NEG_INF = float("-inf")
MASK_BIAS = -1e30


def _split2(a):
    hi = a.astype(BF16)
    lo = (a - hi.astype(F32)).astype(BF16)
    return hi, lo


def _split3(a):
    hi = a.astype(BF16)
    r = a - hi.astype(F32)
    mid = r.astype(BF16)
    lo = (r - mid.astype(F32)).astype(BF16)
    return hi, mid, lo


def _mm(a, b):
    return jnp.dot(a.astype(BF16), b.astype(BF16), preferred_element_type=F32)


def _mm_nt(a, b):
    return lax.dot_general(a.astype(BF16), b.astype(BF16), (((1,), (1,)), ((), ())),
                           preferred_element_type=F32)


def _mm_sel_rhs(a, e, parts=3):
    eb = e.astype(BF16)
    pieces = _split3(a) if parts == 3 else _split2(a)
    out = jnp.dot(pieces[0], eb, preferred_element_type=F32)
    for p in pieces[1:]:
        out = out + jnp.dot(p, eb, preferred_element_type=F32)
    return out


def _mm_sel_lhs(e, b):
    eb = e.astype(BF16)
    pieces = _split3(b)
    out = jnp.dot(eb, pieces[0], preferred_element_type=F32)
    for p in pieces[1:]:
        out = out + jnp.dot(eb, p, preferred_element_type=F32)
    return out


def _mm_nt_hp(a, b):
    ah, al = _split2(a)
    bh, bl = _split2(b)
    dn = (((1,), (1,)), ((), ()))
    return (lax.dot_general(ah, bh, dn, preferred_element_type=F32)
            + lax.dot_general(ah, bl, dn, preferred_element_type=F32)
            + lax.dot_general(al, bh, dn, preferred_element_type=F32))


def _rows8(row):
    return jnp.broadcast_to(row, (8, row.shape[1]))


def _row_sel(row, e):
    return _mm_sel_rhs(_rows8(row), e)[0:1, :]


def _silu(x):
    return x * jax.nn.sigmoid(x)


def _softplus(x):
    return jnp.maximum(x, 0.0) + jnp.log1p(jnp.exp(-jnp.abs(x)))


def _rope128(x, c, s1, s2, r):
    return x * c + pltpu.roll(x, LANES - r, 1) * s1 + pltpu.roll(x, r, 1) * s2


def _seg_matrix(n_rows, n_cols, seg, transpose=False):
    if not transpose:
        lane = lax.broadcasted_iota(jnp.int32, (n_rows, n_cols), 0)
        sid = lax.broadcasted_iota(jnp.int32, (n_rows, n_cols), 1)
    else:
        sid = lax.broadcasted_iota(jnp.int32, (n_rows, n_cols), 0)
        lane = lax.broadcasted_iota(jnp.int32, (n_rows, n_cols), 1)
    return jnp.where(lane // seg == sid, 1.0, 0.0).astype(F32)


def _const_spec(shape):
    nd = len(shape)
    return pl.BlockSpec(shape, lambda *_: (0,) * nd, pipeline_mode=pl.Buffered(1))


def _params(n_grid):
    return pltpu.CompilerParams(dimension_semantics=("arbitrary",) * n_grid,
                                vmem_limit_bytes=VMEM_LIMIT)


_PROJ_CHUNK = 512


def _proj_into(xb, w_ref, o_ref, width, rope=None):
    for c0 in range(0, width, _PROJ_CHUNK):
        cw = min(_PROJ_CHUNK, width - c0)
        r = jnp.dot(xb, w_ref[:, c0:c0 + cw], preferred_element_type=F32)
        if rope is None:
            o_ref[:, c0:c0 + cw] = r
        else:
            c, s1, s2, shift = rope
            for j in range(cw // LANES):
                o_ref[:, c0 + j * LANES:c0 + (j + 1) * LANES] = _rope128(
                    r[:, j * LANES:(j + 1) * LANES], c, s1, s2, shift)


def _kv_slot(m, width, tm, slot):
    if slot is None:
        spec = pl.BlockSpec((tm, width), lambda i: (i, 0))
        return jax.ShapeDtypeStruct((m, width), F32), spec, []
    n_layers, li, prev = slot
    spec = pl.BlockSpec((None, tm, width), lambda i: (li, i, 0))
    return jax.ShapeDtypeStruct((n_layers, m, width), F32), spec, list(prev or [])


def _inproj_even_kernel(n_alias, x_ref, c_ref, s1_ref, s2_ref, wz, wxbc, wdt, wdtT, wq, wk, wv, wg,
                        dtb, dtbT, *rest):
    z_o, xbc_o, dt_o, dtT_o, q_o, k_o, v_o, g_o = rest[n_alias:]
    xb = x_ref[...].astype(BF16)
    rope = (c_ref[...], s1_ref[...], s2_ref[...], ROT_B // 2)
    _proj_into(xb, wz, z_o, D_INNER_A)
    _proj_into(xb, wxbc, xbc_o, CONV_DIM)
    dt_o[...] = _softplus(jnp.dot(xb, wdt[...], preferred_element_type=F32) + dtb[...])
    dtT_o[...] = _softplus(
        lax.dot_general(wdtT[...], xb, (((1,), (1,)), ((), ())), preferred_element_type=F32)
        + dtbT[...])
    _proj_into(xb, wq, q_o, D_INNER_B, rope)
    _proj_into(xb, wk, k_o, D_INNER_B, rope)
    _proj_into(xb, wv, v_o, D_INNER_B)
    _proj_into(xb, wg, g_o, D_INNER_B)


def _inproj_even(x, tables, w_in, dt_bias, tm, kv_slot=None):
    m = x.shape[0]
    nt = tables[0].shape[0] // tm
    o = [0]
    for s in (D_INNER_A, CONV_DIM, H_A, D_INNER_B, D_INNER_B, D_INNER_B, D_INNER_B):
        o.append(o[-1] + s)
    wb = w_in.astype(BF16)
    wz, wxbc, wdt, wq, wk, wv, wg = [wb[:, o[i]:o[i + 1]] for i in range(7)]
    wdtT = wdt.T
    dtb = dt_bias.astype(F32).reshape(1, H_A)
    dtbT = dt_bias.astype(F32).reshape(H_A, 1)
    row = lambda w: pl.BlockSpec((tm, w), lambda i: (i, 0))
    tab = pl.BlockSpec((tm, LANES), lambda i: (i % nt, 0))
    consts = [wz, wxbc, wdt, wdtT, wq, wk, wv, wg, dtb, dtbT]
    kv_shape, kv_spec, prev = _kv_slot(m, D_INNER_B, tm, kv_slot)
    plain = jax.ShapeDtypeStruct((m, D_INNER_B), F32)
    out_shape = [jax.ShapeDtypeStruct((m, D_INNER_A), F32), jax.ShapeDtypeStruct((m, CONV_DIM), F32),
                 jax.ShapeDtypeStruct((m, H_A), F32), jax.ShapeDtypeStruct((H_A, m), F32),
                 plain, kv_shape, kv_shape, plain]
    out_specs = [row(D_INNER_A), row(CONV_DIM), row(H_A), pl.BlockSpec((H_A, tm), lambda i: (0, i)),
                 row(D_INNER_B), kv_spec, kv_spec, row(D_INNER_B)]
    n_in = 4 + len(consts)
    return pl.pallas_call(
        functools.partial(_inproj_even_kernel, len(prev)), grid=(m // tm,),
        in_specs=[row(D_MODEL), tab, tab, tab] + [_const_spec(c.shape) for c in consts]
                 + [pl.BlockSpec(memory_space=pl.ANY)] * len(prev),
        out_specs=out_specs, out_shape=out_shape, compiler_params=_params(1),
        input_output_aliases={n_in + a: 5 + a for a in range(len(prev))},
        name="inproj_even")(x, *tables, *consts, *prev)


def _inproj_odd_kernel(n_alias, x_ref, c_ref, s1_ref, s2_ref, wq, wk, wv, wg, *rest):
    q_o, k_o, v_o, g_o = rest[n_alias:]
    xb = x_ref[...].astype(BF16)
    rope = (c_ref[...], s1_ref[...], s2_ref[...], ROT_C // 2)
    _proj_into(xb, wq, q_o, D_INNER_C, rope)
    _proj_into(xb, wk, k_o, D_INNER_C, rope)
    _proj_into(xb, wv, v_o, D_INNER_C)
    _proj_into(xb, wg, g_o, D_INNER_C)


def _inproj_odd(x, tables, w_in, tm, kv_slot=None):
    m = x.shape[0]
    nt = tables[0].shape[0] // tm
    wb = w_in.astype(BF16)
    ws = [wb[:, i * D_INNER_C:(i + 1) * D_INNER_C] for i in range(4)]
    row = lambda w: pl.BlockSpec((tm, w), lambda i: (i, 0))
    tab = pl.BlockSpec((tm, LANES), lambda i: (i % nt, 0))
    kv_shape, kv_spec, prev = _kv_slot(m, D_INNER_C, tm, kv_slot)
    plain = jax.ShapeDtypeStruct((m, D_INNER_C), F32)
    n_in = 4 + len(ws)
    return pl.pallas_call(
        functools.partial(_inproj_odd_kernel, len(prev)), grid=(m // tm,),
        in_specs=[row(D_MODEL), tab, tab, tab] + [_const_spec(w.shape) for w in ws]
                 + [pl.BlockSpec(memory_space=pl.ANY)] * len(prev),
        out_specs=[row(D_INNER_C), kv_spec, kv_spec, row(D_INNER_C)],
        out_shape=[plain, kv_shape, kv_shape, plain],
        input_output_aliases={n_in + a: 1 + a for a in range(len(prev))},
        compiler_params=_params(1), name="inproj_odd")(x, *tables, *ws, *prev)


def _post_kernel(n_y, *refs):
    y_refs = refs[:n_y]
    w_refs = refs[n_y:2 * n_y]
    x_ref, p_ref, g_ref, b_ref, wg_ref, wp_ref, o_ref = refs[2 * n_y:]
    out = _mm(y_refs[0][...], w_refs[0][...])
    for y, w in zip(y_refs[1:], w_refs[1:]):
        out = out + _mm(y[...], w[...])
    h = ALPHA * x_ref[...] + out
    hc = h - jnp.mean(h, axis=-1, keepdims=True)
    var = jnp.mean(hc * hc, axis=-1, keepdims=True)
    hn = hc * lax.rsqrt(var + EPS) * g_ref[...] + b_ref[...]
    gate = jax.nn.sigmoid(_mm(hn, wg_ref[...]))
    o_ref[...] = hn + gate * _mm(p_ref[...], wp_ref[...])


def _post(ys, w_outs, x, p, ln_g, ln_b, w_gate, w_proj, tm):
    m = x.shape[0]
    n_y = len(ys)
    row = lambda w: pl.BlockSpec((tm, w), lambda i: (i, 0))
    consts = [w.astype(BF16) for w in w_outs]
    tail = [ln_g.astype(F32).reshape(1, D_MODEL), ln_b.astype(F32).reshape(1, D_MODEL),
            w_gate.astype(BF16), w_proj.astype(BF16)]
    return pl.pallas_call(
        functools.partial(_post_kernel, n_y), grid=(m // tm,),
        in_specs=[row(y.shape[1]) for y in ys] + [_const_spec(c.shape) for c in consts]
                 + [row(D_MODEL), row(PLE_DIM)] + [_const_spec(c.shape) for c in tail],
        out_specs=row(D_MODEL), out_shape=jax.ShapeDtypeStruct((m, D_MODEL), F32),
        compiler_params=_params(1), name="outproj_post")(*ys, *consts, x, p, *tail)


def _gated_group_norm(y, z, nw):
    yz = y * _silu(z)
    gw = D_INNER_A // G_A
    outs = []
    for g in range(G_A):
        seg = yz[:, g * gw:(g + 1) * gw]
        ms = jnp.mean(seg * seg, axis=-1, keepdims=True)
        outs.append(seg * lax.rsqrt(ms + EPS) * nw[:, g * gw:(g + 1) * gw])
    return outs


def _ssd_prompt_kernel(z_ref, xbc_ref, dt_ref, dtT_ref, cw_ref, cb_ref, alog_ref, alogT_ref,
                       dexp_ref, nw_ref, ya_ref, st_ref, cbuf, hT):
    q = SSD_CHUNK
    c = pl.program_id(1)

    @pl.when(c == 0)
    def _():
        hT[...] = jnp.zeros_like(hT)
        cbuf[0:8, :] = jnp.zeros((8, CONV_DIM), F32)

    @pl.when(c > 0)
    def _():
        cbuf[0:8, :] = cbuf[q:q + 8, :]

    cbuf[8:q + 8, :] = xbc_ref[...]
    acc = cbuf[8:q + 8, :] * cw_ref[CONV_W - 1:CONV_W, :] + cb_ref[...]
    for s in range(1, CONV_W):
        acc = acc + cbuf[8 - s:q + 8 - s, :] * cw_ref[CONV_W - 1 - s:CONV_W - s, :]
    xbc = _silu(acc)
    xs = xbc[:, :D_INNER_A]
    bm = xbc[:, D_INNER_A:D_INNER_A + G_A * N_A]
    cm = xbc[:, D_INNER_A + G_A * N_A:]

    dt = dt_ref[...]
    dtT = dtT_ref[...]
    a = -jnp.exp(alog_ref[...])
    aT = -jnp.exp(alogT_ref[...])
    ri = lax.broadcasted_iota(jnp.int32, (q, q), 0)
    ci = lax.broadcasted_iota(jnp.int32, (q, q), 1)
    causal = ri >= ci
    acs = _mm_sel_lhs(jnp.where(causal, 1.0, 0.0), dt * a)
    acsT = _mm_sel_rhs(dtT * aT, jnp.where(ri <= ci, 1.0, 0.0))
    acs_end = acs[q - 1:q, :]
    e_hp = _seg_matrix(H_A, D_INNER_A, P_A, transpose=True)
    w_exp = _mm_sel_rhs(jnp.exp(acs_end - acs) * dt, e_hp)
    eacs_exp = _mm_sel_rhs(jnp.exp(acs), e_hp)
    cd_exp = _row_sel(jnp.exp(acs_end), e_hp)
    xw = xs * w_exp

    lane = lax.broadcasted_iota(jnp.int32, (q, LANES), 1)
    rep = H_A // G_A
    cb_g, bT_g = [], []
    for g in range(G_A):
        b_g = bm[:, g * N_A:(g + 1) * N_A]
        c_g = cm[:, g * N_A:(g + 1) * N_A]
        cb_g.append(_mm_nt(c_g, b_g))
        bT_g.append(b_g.T)

    for k in range(H_A // 2):
        g = (2 * k) // rep
        sl = slice(k * LANES, (k + 1) * LANES)
        ws = []
        for hh in (2 * k, 2 * k + 1):
            seg = acs[:, hh:hh + 1] - acsT[hh:hh + 1, :]
            dec = jnp.exp(jnp.where(causal, seg, NEG_INF))
            ws.append((cb_g[g] * dec * dtT[hh:hh + 1, :]).astype(BF16))
        xp = xs[:, sl]
        rhs = jnp.concatenate([jnp.where(lane < P_A, xp, 0.0), jnp.where(lane >= P_A, xp, 0.0)], axis=0)
        y_diag = jnp.dot(jnp.concatenate(ws, axis=1), rhs.astype(BF16), preferred_element_type=F32)
        h_in = hT[:, sl]
        y_off = _mm(cm[:, g * N_A:(g + 1) * N_A], h_in) * eacs_exp[:, sl]
        ya_ref[:, sl] = y_diag + y_off + dexp_ref[:, sl] * xp
        hT[:, sl] = h_in * cd_exp[:, sl] + _mm(bT_g[g], xw[:, sl])

    outs = _gated_group_norm(ya_ref[...], z_ref[...], nw_ref[...])
    gw = D_INNER_A // G_A
    for g in range(G_A):
        ya_ref[:, g * gw:(g + 1) * gw] = outs[g]

    @pl.when(c == pl.num_programs(1) - 1)
    def _():
        st_ref[...] = hT[...]


def _ssd_prompt(z, xbc, dt, dtT, conv_w, conv_b, a_log, d_skip, norm_w, batch, length):
    q = SSD_CHUNK
    nc = length // q
    blk = lambda w: pl.BlockSpec((q, w), lambda b, c: (b * nc + c, 0))
    consts = [conv_w.astype(F32), conv_b.astype(F32).reshape(1, CONV_DIM),
              a_log.astype(F32).reshape(1, H_A), a_log.astype(F32).reshape(H_A, 1),
              jnp.repeat(d_skip.astype(F32), P_A).reshape(1, D_INNER_A),
              norm_w.astype(F32).reshape(1, D_INNER_A)]
    ya, st = pl.pallas_call(
        _ssd_prompt_kernel, grid=(batch, nc),
        in_specs=[blk(D_INNER_A), blk(CONV_DIM), blk(H_A),
                  pl.BlockSpec((H_A, q), lambda b, c: (0, b * nc + c))]
                 + [_const_spec(cc.shape) for cc in consts],
        out_specs=[blk(D_INNER_A), pl.BlockSpec((None, N_A, D_INNER_A), lambda b, c: (b, 0, 0))],
        out_shape=[jax.ShapeDtypeStruct((batch * length, D_INNER_A), F32),
                   jax.ShapeDtypeStruct((batch, N_A, D_INNER_A), F32)],
        scratch_shapes=[pltpu.VMEM((q + 8, CONV_DIM), F32), pltpu.VMEM((N_A, D_INNER_A), F32)],
        compiler_params=_params(2), name="ssd_prompt")(z, xbc, dt, dtT, *consts)
    state = st.reshape(batch, N_A, H_A, P_A).transpose(0, 2, 3, 1)
    return ya, state


def _dec_conv_kernel(c0_ref, xbc_ref, cw_ref, cb_ref, o_ref):
    acc = xbc_ref[...] * cw_ref[CONV_W - 1:CONV_W, :] + cb_ref[...]
    for tap in range(CONV_W - 1):
        acc = acc + c0_ref[tap] * cw_ref[tap:tap + 1, :]
    o_ref[...] = _silu(acc)


def _dec_state_kernel(h_ref, x_ref, b_ref, c_ref, dt_ref, alog_ref, hn_ref, y_ref):
    dt = dt_ref[...]
    decay = jnp.exp(dt * (-jnp.exp(alog_ref[...])))
    hn = h_ref[...] * decay + (dt * x_ref[...]) * b_ref[...]
    hn_ref[...] = hn
    y_ref[...] = jnp.sum(hn * c_ref[...], axis=-1, keepdims=True)


def _dec_norm_kernel(y_ref, xs_ref, z_ref, dexp_ref, nw_ref, o_ref):
    y = y_ref[...] + dexp_ref[...] * xs_ref[...]
    outs = _gated_group_norm(y, z_ref[...], nw_ref[...])
    gw = D_INNER_A // G_A
    for g in range(G_A):
        o_ref[:, g * gw:(g + 1) * gw] = outs[g]


def _ssd_decode(z, xbc, dt, state_all, li, conv0, conv_w, conv_b, a_log, d_skip, norm_w):
    s = z.shape[0]
    full = lambda shape: pl.BlockSpec(shape, lambda *_: (0,) * len(shape))
    c0t = conv0.transpose(1, 0, 2)
    cw = conv_w.astype(F32)
    cb = conv_b.astype(F32).reshape(1, CONV_DIM)
    act = pl.pallas_call(
        _dec_conv_kernel, grid=(1,),
        in_specs=[full(c0t.shape), full(xbc.shape), full(cw.shape), full(cb.shape)],
        out_specs=full(xbc.shape), out_shape=jax.ShapeDtypeStruct(xbc.shape, F32),
        compiler_params=_params(1), name="ssd_decode_conv")(c0t, xbc, cw, cb)
    xs = act[:, :D_INNER_A]
    rep = H_A // G_A
    x4 = xs.reshape(s, H_A, P_A, 1)
    b4 = jnp.repeat(act[:, D_INNER_A:D_INNER_A + G_A * N_A].reshape(s, G_A, 1, N_A), rep, axis=1)
    c4 = jnp.repeat(act[:, D_INNER_A + G_A * N_A:].reshape(s, G_A, 1, N_A), rep, axis=1)
    dt4 = dt.reshape(s, H_A, 1, 1)
    alog4 = a_log.astype(F32).reshape(1, H_A, 1, 1)
    ts = 8
    seq = lambda a, b: pl.BlockSpec((ts, H_A, a, b), lambda i: (i, 0, 0, 0))
    hn, y4 = pl.pallas_call(
        _dec_state_kernel, grid=(s // ts,),
        in_specs=[pl.BlockSpec((None, ts, H_A, P_A, N_A), lambda i: (li, i, 0, 0, 0)), seq(P_A, 1), seq(1, N_A), seq(1, N_A), seq(1, 1),
                  pl.BlockSpec((1, H_A, 1, 1), lambda i: (0, 0, 0, 0))],
        out_specs=[seq(P_A, N_A), seq(P_A, 1)],
        out_shape=[jax.ShapeDtypeStruct((s, H_A, P_A, N_A), F32),
                   jax.ShapeDtypeStruct((s, H_A, P_A, 1), F32)],
        compiler_params=_params(1), name="ssd_decode_state")(state_all, x4, b4, c4, dt4, alog4)
    y = y4.reshape(s, D_INNER_A)
    dexp = jnp.repeat(d_skip.astype(F32), P_A).reshape(1, D_INNER_A)
    nw = norm_w.astype(F32).reshape(1, D_INNER_A)
    ya = pl.pallas_call(
        _dec_norm_kernel, grid=(1,),
        in_specs=[full(y.shape), full(xs.shape), full(z.shape), full(dexp.shape), full(nw.shape)],
        out_specs=full(y.shape), out_shape=jax.ShapeDtypeStruct(y.shape, F32),
        compiler_params=_params(1), name="ssd_decode_norm")(y, xs, z, dexp, nw)
    conv_new = jnp.concatenate([conv0[:, 1:], xbc[:, None, :]], axis=1)
    return ya, hn, conv_new


def _flash_next(st, h, s_t, v_t):
    m_s, l_s, acc_s = st
    m = m_s[h]
    m_new = jnp.maximum(m, jnp.max(s_t, axis=0, keepdims=True))
    alpha = jnp.exp(m - m_new)
    p = jnp.exp(s_t - m_new)
    m_s[h] = m_new
    l_s[h] = alpha * l_s[h] + jnp.sum(p, axis=0, keepdims=True)
    acc_s[h] = alpha * acc_s[h] + jnp.dot(v_t, p.astype(BF16), preferred_element_type=F32)


def _stage_kv(nb, h, width, k_ref, v_ref, kb, v_t):
    t = ATTN_TILE
    means = []
    for j in range(nb):
        kj = k_ref[j * t:(j + 1) * t, h * width:(h + 1) * width]
        kb[h, j * t:(j + 1) * t, :] = kj.astype(BF16)
        v_t[h, j] = v_ref[j * t:(j + 1) * t, h * width:(h + 1) * width].T.astype(BF16)
        means.append(jnp.mean(kj, axis=0, keepdims=True))
    return means


def _chunk_keys(kb, h, j0):
    t = ATTN_TILE
    return kb[h, pl.ds(pl.multiple_of(j0 * t, t), ATTN_CHUNK * t), :]


def _causal_chunk_mask(qi, j0, n_cols):
    t = ATTN_TILE
    ri = lax.broadcasted_iota(jnp.int32, (ATTN_CHUNK * t, n_cols), 0)
    ci = lax.broadcasted_iota(jnp.int32, (ATTN_CHUNK * t, n_cols), 1)
    for rep in range(1, n_cols // t):
        ci = jnp.where(ci >= rep * t, ci - t, ci)
    return ri + (j0 - qi) * t <= ci


def _chunked_flash(qi, st, s_buf, v_t, scores):
    m_s, l_s, acc_s = st
    ch = ATTN_CHUNK
    n_full = qi // ch
    j_last = n_full * ch
    for h in range(ATTN_HEADS):
        m_s[h] = jnp.full(m_s.shape[1:], NEG_INF, F32)
        l_s[h] = jnp.zeros(l_s.shape[1:], F32)
        acc_s[h] = jnp.zeros(acc_s.shape[1:], F32)
        s_buf[h, 0] = scores(h, j_last, True)

    def consume(slot, j0):
        for h in range(ATTN_HEADS):
            v = jnp.concatenate([v_t[h, j0 + b] for b in range(ch)], axis=1)
            _flash_next(st, h, s_buf[h, slot], v)

    def body(c, j_prev):
        slot = c % 2
        consume(slot, j_prev)
        for h in range(ATTN_HEADS):
            s_buf[h, 1 - slot] = scores(h, c * ch, False)
        return c * ch

    j_prev = lax.fori_loop(0, n_full, body, j_last)
    consume(n_full % 2, j_prev)


def _flash_scratch(nq, width, n_queries):
    t = ATTN_TILE
    return [pltpu.VMEM((ATTN_HEADS, nq * t, width), BF16), pltpu.VMEM((ATTN_HEADS, nq, width, t), BF16),
            pltpu.VMEM((ATTN_HEADS, 1, n_queries), F32), pltpu.VMEM((ATTN_HEADS, 1, n_queries), F32),
            pltpu.VMEM((ATTN_HEADS, width, n_queries), F32),
            pltpu.VMEM((ATTN_HEADS, 2, ATTN_CHUNK * t, n_queries), F32)]


def _moba_prompt_kernel(nb, q_ref, k_ref, v_ref, g_ref, o_ref, kb, v_t, m_s, l_s, acc_s, s_buf,
                        kmean, qs_s, bias_s):
    t = ATTN_TILE
    qi = pl.program_id(2)
    scale = HD_B ** -0.5
    st = (m_s, l_s, acc_s)

    @pl.when(qi == 0)
    def _():
        for h in range(ATTN_HEADS):
            kmean[h] = jnp.concatenate(_stage_kv(nb, h, HD_B, k_ref, v_ref, kb, v_t), axis=0)

    rowb = lax.broadcasted_iota(jnp.int32, (nb, t), 0)
    rowf = rowb.astype(F32)
    for h in range(ATTN_HEADS):
        q = q_ref[:, h * HD_B:(h + 1) * HD_B]
        qs = (q * scale).astype(BF16)
        gate = jnp.where(rowb < qi, _mm_nt_hp(kmean[h], q), NEG_INF)
        sel = jnp.zeros((nb, t), F32)
        for _ in range(min(MOBA_TOPK, nb)):
            mx = jnp.max(gate, axis=0, keepdims=True)
            first = jnp.min(jnp.where(gate == mx, rowf, float(nb)), axis=0, keepdims=True)
            hit = rowf == first
            sel = jnp.maximum(sel, jnp.where(hit, jnp.where(mx > NEG_INF, 1.0, 0.0), 0.0))
            gate = jnp.where(hit, NEG_INF, gate)
        bias_s[h] = jnp.where(rowb == qi, 0.0, jnp.where(sel > 0.0, 0.0, MASK_BIAS))[:, None, :]
        qs_s[h] = qs

    def scores(h, j0, last):
        s = _mm_nt(_chunk_keys(kb, h, j0), qs_s[h]).reshape(ATTN_CHUNK, t, t) + bias_s[h, pl.ds(j0, ATTN_CHUNK)]
        s = s.reshape(ATTN_CHUNK * t, t)
        return jnp.where(_causal_chunk_mask(qi, j0, t), s, NEG_INF) if last else s

    _chunked_flash(qi, st, s_buf, v_t, scores)
    for h in range(ATTN_HEADS):
        sl = slice(h * HD_B, (h + 1) * HD_B)
        o_ref[:, sl] = (acc_s[h] / l_s[h]).T * _silu(g_ref[:, sl])


def _moba_prompt(q, k, v, li, g, batch, length):
    t = ATTN_TILE
    nq = length // t
    assert H_B % ATTN_HEADS == 0 and nq % ATTN_CHUNK == 0
    w = ATTN_HEADS * HD_B
    blk = pl.BlockSpec((t, w), lambda b, h, i: (b * nq + i, h))
    seq = pl.BlockSpec((None, length, w), lambda b, h, i: (li, b, h))
    return pl.pallas_call(
        functools.partial(_moba_prompt_kernel, nq), grid=(batch, H_B // ATTN_HEADS, nq),
        in_specs=[blk, seq, seq, blk], out_specs=blk,
        out_shape=jax.ShapeDtypeStruct((batch * length, D_INNER_B), F32),
        scratch_shapes=_flash_scratch(nq, HD_B, t) + [pltpu.VMEM((ATTN_HEADS, nq, HD_B), F32),
                                                      pltpu.VMEM((ATTN_HEADS, t, HD_B), BF16),
                                                      pltpu.VMEM((ATTN_HEADS, nq, 1, t), F32)],
        compiler_params=_params(3), name="moba_prompt")(q, k, v, g)


def _lambda(lq1, lk1, lq2, lk2, lam_init):
    return (jnp.exp(jnp.sum(lq1[...] * lk1[...], axis=-1, keepdims=True))
            - jnp.exp(jnp.sum(lq2[...] * lk2[...], axis=-1, keepdims=True)) + lam_init)


def _diff_prompt_kernel(nb, lam_init, q_ref, k_ref, v_ref, g_ref, lq1, lk1, lq2, lk2, sw_ref, o_ref,
                        kb, v_t, m_s, l_s, acc_s, s_buf, q2_s):
    t = ATTN_TILE
    qi = pl.program_id(2)
    scale = DK_C ** -0.5
    st = (m_s, l_s, acc_s)

    @pl.when(qi == 0)
    def _():
        for h in range(ATTN_HEADS):
            _stage_kv(nb, h, DV_C, k_ref, v_ref, kb, v_t)

    lam = _lambda(lq1, lk1, lq2, lk2, lam_init)
    lane = lax.broadcasted_iota(jnp.int32, (t, DV_C), 1)
    for h in range(ATTN_HEADS):
        qs = q_ref[:, h * DV_C:(h + 1) * DV_C] * scale
        q2_s[h] = jnp.concatenate([jnp.where(lane < DK_C, qs, 0.0), jnp.where(lane >= DK_C, qs, 0.0)],
                                  axis=0).astype(BF16)

    def scores(h, j0, last):
        s = _mm_nt(_chunk_keys(kb, h, j0), q2_s[h])
        return jnp.where(_causal_chunk_mask(qi, j0, 2 * t), s, NEG_INF) if last else s

    _chunked_flash(qi, st, s_buf, v_t, scores)
    for h in range(ATTN_HEADS):
        sl = slice(h * DV_C, (h + 1) * DV_C)
        o_t = acc_s[h] / l_s[h]
        o = (o_t[:, :t] - lam * o_t[:, t:]).T
        o = o * lax.rsqrt(jnp.mean(o * o, axis=-1, keepdims=True) + EPS) * sw_ref[...] * (1.0 - lam_init)
        o_ref[:, sl] = o * _silu(g_ref[:, sl])


def _diff_prompt(q, k, v, li, g, lams, subln_w, lam_init, batch, length):
    t = ATTN_TILE
    nq = length // t
    assert H_C % ATTN_HEADS == 0 and nq % ATTN_CHUNK == 0
    w = ATTN_HEADS * DV_C
    blk = pl.BlockSpec((t, w), lambda b, h, i: (b * nq + i, h))
    seq = pl.BlockSpec((None, length, w), lambda b, h, i: (li, b, h))
    consts = [x.astype(F32).reshape(1, DK_C) for x in lams] + [subln_w.astype(F32).reshape(1, DV_C)]
    return pl.pallas_call(
        functools.partial(_diff_prompt_kernel, nq, lam_init), grid=(batch, H_C // ATTN_HEADS, nq),
        in_specs=[blk, seq, seq, blk] + [_const_spec(c.shape) for c in consts], out_specs=blk,
        out_shape=jax.ShapeDtypeStruct((batch * length, D_INNER_C), F32),
        scratch_shapes=_flash_scratch(nq, DV_C, 2 * t) + [pltpu.VMEM((ATTN_HEADS, 2 * t, DV_C), BF16)],
        compiler_params=_params(3), name="diff_prompt")(q, k, v, g, *consts)


def _lane_sum_matrix(seg):
    r = lax.broadcasted_iota(jnp.int32, (LANES, LANES), 0)
    c = lax.broadcasted_iota(jnp.int32, (LANES, LANES), 1)
    return jnp.where(r // seg == c // seg, 1.0, 0.0).astype(F32)


def _swap_halves(a):
    return pltpu.roll(a, LANES // 2, a.ndim - 1)


def _page_partial(qs, k_ref, v_ref, jmat, two_maps, want_ksum):
    kk = k_ref[...]
    vv = v_ref[...]
    rows = kk.shape[0] * kk.shape[1]
    s = _mm((kk * qs[None]).reshape(rows, LANES), jmat).reshape(kk.shape)
    m = jnp.max(s, axis=0)
    p = jnp.exp(s - m[None])
    l = jnp.sum(p, axis=0)
    accs = [jnp.sum(p * vv, axis=0)]
    if two_maps:
        p_sw = _swap_halves(p.reshape(rows, LANES)).reshape(kk.shape)
        accs.append(jnp.sum(p_sw * vv, axis=0))
    return m, l, accs, (jnp.sum(kk, axis=0) if want_ksum else None)


def _moba_decode_kernel(n_pages, pt_ref, q_ref, kn_ref, vn_ref, g_ref, *refs):
    k_refs = refs[:n_pages]
    v_refs = refs[n_pages:2 * n_pages]
    o_ref = refs[2 * n_pages]
    scale = HD_B ** -0.5
    ppb = MOBA_BLOCK // PAGE_SIZE
    nb = n_pages // ppb
    ones = _lane_sum_matrix(LANES)
    q = q_ref[0]
    qs = q * scale
    parts = [_page_partial(qs, k_refs[t], v_refs[t], ones, False, True) for t in range(n_pages)]
    kq = [sum(pt[3] for pt in parts[b * ppb:(b + 1) * ppb]) * (1.0 / MOBA_BLOCK) * q for b in range(nb)]
    gates = _mm_sel_rhs(jnp.concatenate(kq, axis=0), ones)
    gate = [gates[b * H_B:(b + 1) * H_B] for b in range(nb)]
    chosen = []
    for b in range(nb):
        rank = jnp.zeros((H_B, LANES), F32)
        for b2 in range(nb):
            if b2 != b:
                beats = (gate[b2] >= gate[b]) if b2 < b else (gate[b2] > gate[b])
                rank = rank + jnp.where(beats, 1.0, 0.0)
        chosen.append(rank < float(MOBA_TOPK))
    s_self = _mm_sel_rhs(qs * kn_ref[0], ones)
    mtot = s_self
    for t, (m, _, _, _) in enumerate(parts):
        mtot = jnp.maximum(mtot, jnp.where(chosen[t // ppb], m, NEG_INF))
    l_tot = jnp.exp(s_self - mtot)
    acc = l_tot * vn_ref[0]
    for t, (m, l, a, _) in enumerate(parts):
        w = jnp.where(chosen[t // ppb], jnp.exp(jnp.minimum(m - mtot, 0.0)), 0.0)
        l_tot = l_tot + w * l
        acc = acc + w * a[0]
    o_ref[0] = (acc / l_tot) * _silu(g_ref[0])


def _diff_decode_kernel(n_pages, lam_init, pt_ref, q_ref, kn_ref, vn_ref, g_ref,
                        lq1, lk1, lq2, lk2, sw_ref, *refs):
    k_refs = refs[:n_pages]
    v_refs = refs[n_pages:2 * n_pages]
    o_ref = refs[2 * n_pages]
    scale = DK_C ** -0.5
    lam = _lambda(lq1, lk1, lq2, lk2, lam_init)
    jmat = _lane_sum_matrix(DK_C)
    qs = q_ref[0] * scale
    parts = [_page_partial(qs, k_refs[t], v_refs[t], jmat, True, False) for t in range(n_pages)]
    s_self = _mm_sel_rhs(qs * kn_ref[0], jmat)
    mtot = s_self
    for m, _, _, _ in parts:
        mtot = jnp.maximum(mtot, m)
    p_self = jnp.exp(s_self - mtot)
    vn = vn_ref[0]
    l_tot = p_self
    acc_a = p_self * vn
    acc_b = _swap_halves(p_self) * vn
    for m, l, a, _ in parts:
        w = jnp.exp(m - mtot)
        l_tot = l_tot + w * l
        acc_a = acc_a + w * a[0]
        acc_b = acc_b + _swap_halves(w) * a[1]
    first = lax.broadcasted_iota(jnp.int32, (H_C, LANES), 1) < DK_C
    l_sw = _swap_halves(l_tot)
    o1 = jnp.where(first, acc_a, acc_b) / jnp.where(first, l_tot, l_sw)
    o2 = jnp.where(first, acc_b, acc_a) / jnp.where(first, l_sw, l_tot)
    o = o1 - lam * o2
    o = o * lax.rsqrt(jnp.mean(o * o, axis=-1, keepdims=True) + EPS) * sw_ref[...] * (1.0 - lam_init)
    o_ref[0] = o * _silu(g_ref[0])


def _decode_attn(kernel_fn, q, kn, vn, g, consts, cache_k, cache_v, li, pt_flat, n_pages, name):
    s = q.shape[0]
    heads, hd = cache_k.shape[3], cache_k.shape[4]
    r3 = lambda a: a.reshape(s, heads, hd)
    row = pl.BlockSpec((1, heads, hd), lambda i, pt: (i, 0, 0))

    def page(t):
        return pl.BlockSpec((None, None, PAGE_SIZE, heads, hd),
                            lambda i, pt: (li, pt[i * n_pages + t], 0, 0, 0))

    const_specs = [pl.BlockSpec(c.shape, lambda i, pt, nd=c.ndim: (0,) * nd) for c in consts]
    grid_spec = pltpu.PrefetchScalarGridSpec(
        num_scalar_prefetch=1, grid=(s,),
        in_specs=[row, row, row, row] + const_specs + [page(t) for t in range(n_pages)] * 2,
        out_specs=row)
    out = pl.pallas_call(
        kernel_fn, grid_spec=grid_spec, out_shape=jax.ShapeDtypeStruct((s, heads, hd), F32),
        compiler_params=_params(1), name=name)(
            pt_flat, r3(q), r3(kn), r3(vn), r3(g), *consts, *([cache_k] * n_pages), *([cache_v] * n_pages))
    return out.reshape(s, heads * hd)


def _rope_tables(pos, rot_dim, period):
    half = rot_dim // 2
    n = pos.shape[0]
    inv = ROPE_THETA ** (-jnp.arange(half, dtype=F32) * (2.0 / rot_dim))
    ang = pos.astype(F32)[:, None] * inv[None, :]
    cos, sin = jnp.cos(ang), jnp.sin(ang)
    c = jnp.concatenate([cos, cos, jnp.ones((n, period - rot_dim), F32)], axis=1)
    s1 = jnp.concatenate([-sin, jnp.zeros((n, period - half), F32)], axis=1)
    s2 = jnp.concatenate([jnp.zeros((n, half), F32), sin, jnp.zeros((n, period - rot_dim), F32)], axis=1)
    reps = LANES // period
    return tuple(jnp.tile(a, (1, reps)) for a in (c, s1, s2))


def kernel(x_prompt, x_sample, cache_moba_k, cache_moba_v, cache_diff_k, cache_diff_v, state_ssm, state_conv, page_table, p_prompt, p_sample, w_in_even, conv_w, conv_b, dt_bias, a_log, d_skip, ssm_norm_w, w_out_even, w_in_odd, lambda_q1, lambda_k1, lambda_q2, lambda_k2, subln_w, w_out_odd, ln_g, ln_b, w_ple_gate, w_ple_proj):
    bp, length, _ = x_prompt.shape
    s = x_sample.shape[0]
    assert x_sample.shape[1] == 1
    n_pages = page_table.shape[1]
    past_len = n_pages * PAGE_SIZE
    assert past_len % MOBA_BLOCK == 0 and length % ATTN_TILE == 0 and length % SSD_CHUNK == 0
    assert s % 8 == 0
    pt_flat = page_table.reshape(-1).astype(jnp.int32)
    tm_p = 256
    xp = x_prompt.reshape(bp * length, D_MODEL)
    xs = x_sample.reshape(s, D_MODEL)
    pos_p = jnp.arange(length, dtype=jnp.int32)
    pos_s = jnp.full((s,), past_len, jnp.int32)
    tabs = {("b", "p"): _rope_tables(pos_p, ROT_B, HD_B), ("b", "s"): _rope_tables(pos_s, ROT_B, HD_B),
            ("c", "p"): _rope_tables(pos_p, ROT_C, DK_C), ("c", "s"): _rope_tables(pos_s, ROT_C, DK_C)}

    n_even, n_odd = (DEPTH + 1) // 2, DEPTH // 2
    pm_kv, pd_kv, p_ssm, p_conv = None, None, [], []
    sm_k, sm_v, sd_k, sd_v, s_ssm, s_conv = [], [], [], [], [], []
    for i in range(DEPTH):
        li = i // 2
        post = functools.partial(_post, ln_g=ln_g[i], ln_b=ln_b[i], w_gate=w_ple_gate[i], w_proj=w_ple_proj[i])
        pp = p_prompt[i].reshape(bp * length, PLE_DIM)
        ps = p_sample[i].reshape(s, PLE_DIM)
        if i % 2 == 0:
            ssm_w = (conv_w[li], conv_b[li], a_log[li], d_skip[li], ssm_norm_w[li])
            w_outs = [w_out_even[li][:D_INNER_A], w_out_even[li][D_INNER_A:]]
            z, xbc, dt, dtT, q, k, v, g = _inproj_even(xp, tabs[("b", "p")], w_in_even[li], dt_bias[li], tm_p,
                                                       kv_slot=(n_even, li, pm_kv))
            pm_kv = (k, v)
            ya, st = _ssd_prompt(z, xbc, dt, dtT, *ssm_w, bp, length)
            yb = _moba_prompt(q, k, v, li, g, bp, length)
            xp = post([ya, yb], w_outs, xp, pp, tm=tm_p)
            p_ssm.append(st)
            p_conv.append(xbc.reshape(bp, length, CONV_DIM)[:, length - (CONV_W - 1):])
            z, xbc, dt, dtT, q, k, v, g = _inproj_even(xs, tabs[("b", "s")], w_in_even[li], dt_bias[li], s)
            ya, st, cv = _ssd_decode(z, xbc, dt, state_ssm, li, state_conv[li], *ssm_w)
            yb = _decode_attn(functools.partial(_moba_decode_kernel, n_pages), q, k, v, g, [],
                              cache_moba_k, cache_moba_v, li, pt_flat, n_pages, "moba_decode")
            xs = post([ya, yb], w_outs, xs, ps, tm=s)
            sm_k.append(k.reshape(s, 1, H_B, HD_B)); sm_v.append(v.reshape(s, 1, H_B, HD_B))
            s_ssm.append(st); s_conv.append(cv)
        else:
            lam_init = 0.8 - 0.6 * math.exp(-0.3 * i)
            lams = (lambda_q1[li], lambda_k1[li], lambda_q2[li], lambda_k2[li])
            q, k, v, g = _inproj_odd(xp, tabs[("c", "p")], w_in_odd[li], tm_p, kv_slot=(n_odd, li, pd_kv))
            pd_kv = (k, v)
            y = _diff_prompt(q, k, v, li, g, lams, subln_w[li], lam_init, bp, length)
            xp = post([y], [w_out_odd[li]], xp, pp, tm=tm_p)
            q, k, v, g = _inproj_odd(xs, tabs[("c", "s")], w_in_odd[li], s)
            consts = [x.astype(F32).reshape(1, DK_C) for x in lams] + \
                     [subln_w[li].astype(F32).reshape(1, DV_C)]
            y = _decode_attn(functools.partial(_diff_decode_kernel, n_pages, lam_init), q, k, v, g, consts,
                             cache_diff_k, cache_diff_v, li, pt_flat, n_pages, "diff_decode")
            xs = post([y], [w_out_odd[li]], xs, ps, tm=s)
            sd_k.append(k.reshape(s, 1, H_C, 2 * DK_C)); sd_v.append(v.reshape(s, 1, H_C, DV_C))
    return (xp.reshape(bp, length, D_MODEL), xs.reshape(s, 1, D_MODEL),
            pm_kv[0].reshape(n_even, bp, length, H_B, HD_B), pm_kv[1].reshape(n_even, bp, length, H_B, HD_B),
            pd_kv[0].reshape(n_odd, bp, length, H_C, 2 * DK_C), pd_kv[1].reshape(n_odd, bp, length, H_C, DV_C),
            jnp.stack(p_ssm), jnp.stack(p_conv),
            jnp.stack(sm_k), jnp.stack(sm_v), jnp.stack(sd_k), jnp.stack(sd_v),
            jnp.stack(s_ssm), jnp.stack(s_conv))
```

```python
import functools
import math

import jax
import jax.numpy as jnp
from jax import lax
from jax.experimental import pallas as pl
from jax.experimental.pallas import tpu as pltpu

F32 = jnp.float32
BF16 = jnp.bfloat16

D_MODEL = 1024
DEPTH = 4
PAGE_SIZE = 128
PLE_DIM = 256
EPS = 1e-5
ROPE_THETA = 500000.0
ALPHA = (2 * DEPTH) ** 0.25

D_INNER_A = D_MODEL
P_A = 64
H_A = D_INNER_A // P_A
G_A = 2
N_A = 128
CONV_W = 4
CONV_DIM = D_INNER_A + 2 * G_A * N_A
SSD_CHUNK = 128

H_B = 8
HD_B = D_MODEL // H_B
D_INNER_B = H_B * HD_B
ROT_B = HD_B // 4
MOBA_BLOCK = 256
MOBA_TOPK = 3

H_C = 8
DK_C = D_MODEL // (2 * H_C)
DV_C = 2 * DK_C
D_INNER_C = H_C * DV_C
ROT_C = DK_C // 4

LANES = 128
ATTN_TILE = 256
ATTN_HEADS = 2
ATTN_CHUNK = 4
VMEM_LIMIT = 56 * 1024 * 1024
NEG_INF = float("-inf")
MASK_BIAS = -1e30


def _split2(a):
    hi = a.astype(BF16)
    lo = (a - hi.astype(F32)).astype(BF16)
    return hi, lo


def _split3(a):
    hi = a.astype(BF16)
    r = a - hi.astype(F32)
    mid = r.astype(BF16)
    lo = (r - mid.astype(F32)).astype(BF16)
    return hi, mid, lo


def _mm(a, b):
    return jnp.dot(a.astype(BF16), b.astype(BF16), preferred_element_type=F32)


def _mm_nt(a, b):
    return lax.dot_general(a.astype(BF16), b.astype(BF16), (((1,), (1,)), ((), ())),
                           preferred_element_type=F32)


def _mm_sel_rhs(a, e, parts=3):
    eb = e.astype(BF16)
    pieces = _split3(a) if parts == 3 else _split2(a)
    out = jnp.dot(pieces[0], eb, preferred_element_type=F32)
    for p in pieces[1:]:
        out = out + jnp.dot(p, eb, preferred_element_type=F32)
    return out


def _mm_sel_lhs(e, b):
    eb = e.astype(BF16)
    pieces = _split3(b)
    out = jnp.dot(eb, pieces[0], preferred_element_type=F32)
    for p in pieces[1:]:
        out = out + jnp.dot(eb, p, preferred_element_type=F32)
    return out


def _mm_nt_hp(a, b):
    ah, al = _split2(a)
    bh, bl = _split2(b)
    dn = (((1,), (1,)), ((), ()))
    return (lax.dot_general(ah, bh, dn, preferred_element_type=F32)
            + lax.dot_general(ah, bl, dn, preferred_element_type=F32)
            + lax.dot_general(al, bh, dn, preferred_element_type=F32))


def _rows8(row):
    return jnp.broadcast_to(row, (8, row.shape[1]))


def _row_sel(row, e):
    return _mm_sel_rhs(_rows8(row), e)[0:1, :]


def _silu(x):
    return x * jax.nn.sigmoid(x)


def _softplus(x):
    return jnp.maximum(x, 0.0) + jnp.log1p(jnp.exp(-jnp.abs(x)))


def _rope128(x, c, s1, s2, r):
    return x * c + pltpu.roll(x, LANES - r, 1) * s1 + pltpu.roll(x, r, 1) * s2


def _seg_matrix(n_rows, n_cols, seg, transpose=False):
    if not transpose:
        lane = lax.broadcasted_iota(jnp.int32, (n_rows, n_cols), 0)
        sid = lax.broadcasted_iota(jnp.int32, (n_rows, n_cols), 1)
    else:
        sid = lax.broadcasted_iota(jnp.int32, (n_rows, n_cols), 0)
        lane = lax.broadcasted_iota(jnp.int32, (n_rows, n_cols), 1)
    return jnp.where(lane // seg == sid, 1.0, 0.0).astype(F32)


def _const_spec(shape):
    nd = len(shape)
    return pl.BlockSpec(shape, lambda *_: (0,) * nd, pipeline_mode=pl.Buffered(1))


def _params(n_grid):
    return pltpu.CompilerParams(dimension_semantics=("arbitrary",) * n_grid,
                                vmem_limit_bytes=VMEM_LIMIT)


_PROJ_CHUNK = 512


def _proj_into(xb, w_ref, o_ref, width, rope=None):
    for c0 in range(0, width, _PROJ_CHUNK):
        cw = min(_PROJ_CHUNK, width - c0)
        r = jnp.dot(xb, w_ref[:, c0:c0 + cw], preferred_element_type=F32)
        if rope is None:
            o_ref[:, c0:c0 + cw] = r
        else:
            c, s1, s2, shift = rope
            for j in range(cw // LANES):
                o_ref[:, c0 + j * LANES:c0 + (j + 1) * LANES] = _rope128(
                    r[:, j * LANES:(j + 1) * LANES], c, s1, s2, shift)


def _kv_slot(m, width, tm, slot):
    if slot is None:
        spec = pl.BlockSpec((tm, width), lambda i: (i, 0))
        return jax.ShapeDtypeStruct((m, width), F32), spec, []
    n_layers, li, prev = slot
    spec = pl.BlockSpec((None, tm, width), lambda i: (li, i, 0))
    return jax.ShapeDtypeStruct((n_layers, m, width), F32), spec, list(prev or [])


def _inproj_even_kernel(n_alias, x_ref, c_ref, s1_ref, s2_ref, wz, wxbc, wdt, wdtT, wq, wk, wv, wg,
                        dtb, dtbT, *rest):
    z_o, xbc_o, dt_o, dtT_o, q_o, k_o, v_o, g_o = rest[n_alias:]
    xb = x_ref[...].astype(BF16)
    rope = (c_ref[...], s1_ref[...], s2_ref[...], ROT_B // 2)
    _proj_into(xb, wz, z_o, D_INNER_A)
    _proj_into(xb, wxbc, xbc_o, CONV_DIM)
    dt_o[...] = _softplus(jnp.dot(xb, wdt[...], preferred_element_type=F32) + dtb[...])
    dtT_o[...] = _softplus(
        lax.dot_general(wdtT[...], xb, (((1,), (1,)), ((), ())), preferred_element_type=F32)
        + dtbT[...])
    _proj_into(xb, wq, q_o, D_INNER_B, rope)
    _proj_into(xb, wk, k_o, D_INNER_B, rope)
    _proj_into(xb, wv, v_o, D_INNER_B)
    _proj_into(xb, wg, g_o, D_INNER_B)


def _inproj_even(x, tables, w_in, dt_bias, tm, kv_slot=None):
    m = x.shape[0]
    nt = tables[0].shape[0] // tm
    o = [0]
    for s in (D_INNER_A, CONV_DIM, H_A, D_INNER_B, D_INNER_B, D_INNER_B, D_INNER_B):
        o.append(o[-1] + s)
    wb = w_in.astype(BF16)
    wz, wxbc, wdt, wq, wk, wv, wg = [wb[:, o[i]:o[i + 1]] for i in range(7)]
    wdtT = wdt.T
    dtb = dt_bias.astype(F32).reshape(1, H_A)
    dtbT = dt_bias.astype(F32).reshape(H_A, 1)
    row = lambda w: pl.BlockSpec((tm, w), lambda i: (i, 0))
    tab = pl.BlockSpec((tm, LANES), lambda i: (i % nt, 0))
    consts = [wz, wxbc, wdt, wdtT, wq, wk, wv, wg, dtb, dtbT]
    kv_shape, kv_spec, prev = _kv_slot(m, D_INNER_B, tm, kv_slot)
    plain = jax.ShapeDtypeStruct((m, D_INNER_B), F32)
    out_shape = [jax.ShapeDtypeStruct((m, D_INNER_A), F32), jax.ShapeDtypeStruct((m, CONV_DIM), F32),
                 jax.ShapeDtypeStruct((m, H_A), F32), jax.ShapeDtypeStruct((H_A, m), F32),
                 plain, kv_shape, kv_shape, plain]
    out_specs = [row(D_INNER_A), row(CONV_DIM), row(H_A), pl.BlockSpec((H_A, tm), lambda i: (0, i)),
                 row(D_INNER_B), kv_spec, kv_spec, row(D_INNER_B)]
    n_in = 4 + len(consts)
    return pl.pallas_call(
        functools.partial(_inproj_even_kernel, len(prev)), grid=(m // tm,),
        in_specs=[row(D_MODEL), tab, tab, tab] + [_const_spec(c.shape) for c in consts]
                 + [pl.BlockSpec(memory_space=pl.ANY)] * len(prev),
        out_specs=out_specs, out_shape=out_shape, compiler_params=_params(1),
        input_output_aliases={n_in + a: 5 + a for a in range(len(prev))},
        name="inproj_even")(x, *tables, *consts, *prev)


def _inproj_odd_kernel(n_alias, x_ref, c_ref, s1_ref, s2_ref, wq, wk, wv, wg, *rest):
    q_o, k_o, v_o, g_o = rest[n_alias:]
    xb = x_ref[...].astype(BF16)
    rope = (c_ref[...], s1_ref[...], s2_ref[...], ROT_C // 2)
    _proj_into(xb, wq, q_o, D_INNER_C, rope)
    _proj_into(xb, wk, k_o, D_INNER_C, rope)
    _proj_into(xb, wv, v_o, D_INNER_C)
    _proj_into(xb, wg, g_o, D_INNER_C)


def _inproj_odd(x, tables, w_in, tm, kv_slot=None):
    m = x.shape[0]
    nt = tables[0].shape[0] // tm
    wb = w_in.astype(BF16)
    ws = [wb[:, i * D_INNER_C:(i + 1) * D_INNER_C] for i in range(4)]
    row = lambda w: pl.BlockSpec((tm, w), lambda i: (i, 0))
    tab = pl.BlockSpec((tm, LANES), lambda i: (i % nt, 0))
    kv_shape, kv_spec, prev = _kv_slot(m, D_INNER_C, tm, kv_slot)
    plain = jax.ShapeDtypeStruct((m, D_INNER_C), F32)
    n_in = 4 + len(ws)
    return pl.pallas_call(
        functools.partial(_inproj_odd_kernel, len(prev)), grid=(m // tm,),
        in_specs=[row(D_MODEL), tab, tab, tab] + [_const_spec(w.shape) for w in ws]
                 + [pl.BlockSpec(memory_space=pl.ANY)] * len(prev),
        out_specs=[row(D_INNER_C), kv_spec, kv_spec, row(D_INNER_C)],
        out_shape=[plain, kv_shape, kv_shape, plain],
        input_output_aliases={n_in + a: 1 + a for a in range(len(prev))},
        compiler_params=_params(1), name="inproj_odd")(x, *tables, *ws, *prev)


def _post_kernel(n_y, *refs):
    y_refs = refs[:n_y]
    w_refs = refs[n_y:2 * n_y]
    x_ref, p_ref, g_ref, b_ref, wg_ref, wp_ref, o_ref = refs[2 * n_y:]
    out = _mm(y_refs[0][...], w_refs[0][...])
    for y, w in zip(y_refs[1:], w_refs[1:]):
        out = out + _mm(y[...], w[...])
    h = ALPHA * x_ref[...] + out
    hc = h - jnp.mean(h, axis=-1, keepdims=True)
    var = jnp.mean(hc * hc, axis=-1, keepdims=True)
    hn = hc * lax.rsqrt(var + EPS) * g_ref[...] + b_ref[...]
    gate = jax.nn.sigmoid(_mm(hn, wg_ref[...]))
    o_ref[...] = hn + gate * _mm(p_ref[...], wp_ref[...])


def _post(ys, w_outs, x, p, ln_g, ln_b, w_gate, w_proj, tm):
    m = x.shape[0]
    n_y = len(ys)
    row = lambda w: pl.BlockSpec((tm, w), lambda i: (i, 0))
    consts = [w.astype(BF16) for w in w_outs]
    tail = [ln_g.astype(F32).reshape(1, D_MODEL), ln_b.astype(F32).reshape(1, D_MODEL),
            w_gate.astype(BF16), w_proj.astype(BF16)]
    return pl.pallas_call(
        functools.partial(_post_kernel, n_y), grid=(m // tm,),
        in_specs=[row(y.shape[1]) for y in ys] + [_const_spec(c.shape) for c in consts]
                 + [row(D_MODEL), row(PLE_DIM)] + [_const_spec(c.shape) for c in tail],
        out_specs=row(D_MODEL), out_shape=jax.ShapeDtypeStruct((m, D_MODEL), F32),
        compiler_params=_params(1), name="outproj_post")(*ys, *consts, x, p, *tail)


def _gated_group_norm(y, z, nw):
    yz = y * _silu(z)
    gw = D_INNER_A // G_A
    outs = []
    for g in range(G_A):
        seg = yz[:, g * gw:(g + 1) * gw]
        ms = jnp.mean(seg * seg, axis=-1, keepdims=True)
        outs.append(seg * lax.rsqrt(ms + EPS) * nw[:, g * gw:(g + 1) * gw])
    return outs


def _ssd_prompt_kernel(z_ref, xbc_ref, dt_ref, dtT_ref, cw_ref, cb_ref, alog_ref, alogT_ref,
                       dexp_ref, nw_ref, ya_ref, st_ref, cbuf, hT):
    q = SSD_CHUNK
    c = pl.program_id(1)

    @pl.when(c == 0)
    def _():
        hT[...] = jnp.zeros_like(hT)
        cbuf[0:8, :] = jnp.zeros((8, CONV_DIM), F32)

    @pl.when(c > 0)
    def _():
        cbuf[0:8, :] = cbuf[q:q + 8, :]

    cbuf[8:q + 8, :] = xbc_ref[...]
    acc = cbuf[8:q + 8, :] * cw_ref[CONV_W - 1:CONV_W, :] + cb_ref[...]
    for s in range(1, CONV_W):
        acc = acc + cbuf[8 - s:q + 8 - s, :] * cw_ref[CONV_W - 1 - s:CONV_W - s, :]
    xbc = _silu(acc)
    xs = xbc[:, :D_INNER_A]
    bm = xbc[:, D_INNER_A:D_INNER_A + G_A * N_A]
    cm = xbc[:, D_INNER_A + G_A * N_A:]

    dt = dt_ref[...]
    dtT = dtT_ref[...]
    a = -jnp.exp(alog_ref[...])
    aT = -jnp.exp(alogT_ref[...])
    ri = lax.broadcasted_iota(jnp.int32, (q, q), 0)
    ci = lax.broadcasted_iota(jnp.int32, (q, q), 1)
    causal = ri >= ci
    acs = _mm_sel_lhs(jnp.where(causal, 1.0, 0.0), dt * a)
    acsT = _mm_sel_rhs(dtT * aT, jnp.where(ri <= ci, 1.0, 0.0))
    acs_end = acs[q - 1:q, :]
    e_hp = _seg_matrix(H_A, D_INNER_A, P_A, transpose=True)
    w_exp = _mm_sel_rhs(jnp.exp(acs_end - acs) * dt, e_hp)
    eacs_exp = _mm_sel_rhs(jnp.exp(acs), e_hp)
    cd_exp = _row_sel(jnp.exp(acs_end), e_hp)
    xw = xs * w_exp

    lane = lax.broadcasted_iota(jnp.int32, (q, LANES), 1)
    rep = H_A // G_A
    cb_g, bT_g = [], []
    for g in range(G_A):
        b_g = bm[:, g * N_A:(g + 1) * N_A]
        c_g = cm[:, g * N_A:(g + 1) * N_A]
        cb_g.append(_mm_nt(c_g, b_g))
        bT_g.append(b_g.T)

    for k in range(H_A // 2):
        g = (2 * k) // rep
        sl = slice(k * LANES, (k + 1) * LANES)
        ws = []
        for hh in (2 * k, 2 * k + 1):
            seg = acs[:, hh:hh + 1] - acsT[hh:hh + 1, :]
            dec = jnp.exp(jnp.where(causal, seg, NEG_INF))
            ws.append((cb_g[g] * dec * dtT[hh:hh + 1, :]).astype(BF16))
        xp = xs[:, sl]
        rhs = jnp.concatenate([jnp.where(lane < P_A, xp, 0.0), jnp.where(lane >= P_A, xp, 0.0)], axis=0)
        y_diag = jnp.dot(jnp.concatenate(ws, axis=1), rhs.astype(BF16), preferred_element_type=F32)
        h_in = hT[:, sl]
        y_off = _mm(cm[:, g * N_A:(g + 1) * N_A], h_in) * eacs_exp[:, sl]
        ya_ref[:, sl] = y_diag + y_off + dexp_ref[:, sl] * xp
        hT[:, sl] = h_in * cd_exp[:, sl] + _mm(bT_g[g], xw[:, sl])

    outs = _gated_group_norm(ya_ref[...], z_ref[...], nw_ref[...])
    gw = D_INNER_A // G_A
    for g in range(G_A):
        ya_ref[:, g * gw:(g + 1) * gw] = outs[g]

    @pl.when(c == pl.num_programs(1) - 1)
    def _():
        st_ref[...] = hT[...]


def _ssd_prompt(z, xbc, dt, dtT, conv_w, conv_b, a_log, d_skip, norm_w, batch, length):
    q = SSD_CHUNK
    nc = length // q
    blk = lambda w: pl.BlockSpec((q, w), lambda b, c: (b * nc + c, 0))
    consts = [conv_w.astype(F32), conv_b.astype(F32).reshape(1, CONV_DIM),
              a_log.astype(F32).reshape(1, H_A), a_log.astype(F32).reshape(H_A, 1),
              jnp.repeat(d_skip.astype(F32), P_A).reshape(1, D_INNER_A),
              norm_w.astype(F32).reshape(1, D_INNER_A)]
    ya, st = pl.pallas_call(
        _ssd_prompt_kernel, grid=(batch, nc),
        in_specs=[blk(D_INNER_A), blk(CONV_DIM), blk(H_A),
                  pl.BlockSpec((H_A, q), lambda b, c: (0, b * nc + c))]
                 + [_const_spec(cc.shape) for cc in consts],
        out_specs=[blk(D_INNER_A), pl.BlockSpec((None, N_A, D_INNER_A), lambda b, c: (b, 0, 0))],
        out_shape=[jax.ShapeDtypeStruct((batch * length, D_INNER_A), F32),
                   jax.ShapeDtypeStruct((batch, N_A, D_INNER_A), F32)],
        scratch_shapes=[pltpu.VMEM((q + 8, CONV_DIM), F32), pltpu.VMEM((N_A, D_INNER_A), F32)],
        compiler_params=_params(2), name="ssd_prompt")(z, xbc, dt, dtT, *consts)
    state = st.reshape(batch, N_A, H_A, P_A).transpose(0, 2, 3, 1)
    return ya, state


def _dec_conv_kernel(c0_ref, xbc_ref, cw_ref, cb_ref, o_ref):
    acc = xbc_ref[...] * cw_ref[CONV_W - 1:CONV_W, :] + cb_ref[...]
    for tap in range(CONV_W - 1):
        acc = acc + c0_ref[tap] * cw_ref[tap:tap + 1, :]
    o_ref[...] = _silu(acc)


def _dec_state_kernel(n_alias, h_ref, x_ref, b_ref, c_ref, dt_ref, alog_ref, *rest):
    hn_ref, y_ref = rest[n_alias:]
    dt = dt_ref[...]
    decay = jnp.exp(dt * (-jnp.exp(alog_ref[...])))
    x = x_ref[...][..., None]
    hn = h_ref[...] * decay + (dt * x) * b_ref[...]
    hn_ref[...] = hn
    y_ref[...] = jnp.sum(hn * c_ref[...], axis=-1)


def _dec_norm_kernel(y_ref, xs_ref, z_ref, dexp_ref, nw_ref, o_ref):
    y = y_ref[...] + dexp_ref[...] * xs_ref[...]
    outs = _gated_group_norm(y, z_ref[...], nw_ref[...])
    gw = D_INNER_A // G_A
    for g in range(G_A):
        o_ref[:, g * gw:(g + 1) * gw] = outs[g]


def _ssd_decode(z, xbc, dt, state_all, li, prev_states, conv0, conv_w, conv_b, a_log, d_skip, norm_w):
    s = z.shape[0]
    full = lambda shape: pl.BlockSpec(shape, lambda *_: (0,) * len(shape))
    c0t = conv0.transpose(1, 0, 2)
    cw = conv_w.astype(F32)
    cb = conv_b.astype(F32).reshape(1, CONV_DIM)
    act = pl.pallas_call(
        _dec_conv_kernel, grid=(1,),
        in_specs=[full(c0t.shape), full(xbc.shape), full(cw.shape), full(cb.shape)],
        out_specs=full(xbc.shape), out_shape=jax.ShapeDtypeStruct(xbc.shape, F32),
        compiler_params=_params(1), name="ssd_decode_conv")(c0t, xbc, cw, cb)
    xs = act[:, :D_INNER_A]
    rep = H_A // G_A
    x3 = xs.reshape(s, H_A, P_A)
    b4 = jnp.repeat(act[:, D_INNER_A:D_INNER_A + G_A * N_A].reshape(s, G_A, 1, N_A), rep, axis=1)
    c4 = jnp.repeat(act[:, D_INNER_A + G_A * N_A:].reshape(s, G_A, 1, N_A), rep, axis=1)
    dt4 = dt.reshape(s, H_A, 1, 1)
    alog4 = a_log.astype(F32).reshape(1, H_A, 1, 1)
    ts = 8
    seq = lambda a, b: pl.BlockSpec((ts, H_A, a, b), lambda i: (i, 0, 0, 0))
    dense = pl.BlockSpec((ts, H_A, P_A), lambda i: (i, 0, 0))
    prev = list(prev_states or [])
    hn, y3 = pl.pallas_call(
        functools.partial(_dec_state_kernel, len(prev)), grid=(s // ts,),
        in_specs=[pl.BlockSpec((None, ts, H_A, P_A, N_A), lambda i: (li, i, 0, 0, 0)), dense, seq(1, N_A), seq(1, N_A), seq(1, 1),
                  pl.BlockSpec((1, H_A, 1, 1), lambda i: (0, 0, 0, 0))]
                 + [pl.BlockSpec(memory_space=pl.ANY)] * len(prev),
        out_specs=[pl.BlockSpec((None, ts, H_A, P_A, N_A), lambda i: (li, i, 0, 0, 0)), dense],
        input_output_aliases={6 + a: a for a in range(len(prev))},
        out_shape=[jax.ShapeDtypeStruct(state_all.shape, F32),
                   jax.ShapeDtypeStruct((s, H_A, P_A), F32)],
        compiler_params=_params(1), name="ssd_decode_state")(state_all, x3, b4, c4, dt4, alog4, *prev)
    y = y3.reshape(s, D_INNER_A)
    dexp = jnp.repeat(d_skip.astype(F32), P_A).reshape(1, D_INNER_A)
    nw = norm_w.astype(F32).reshape(1, D_INNER_A)
    ya = pl.pallas_call(
        _dec_norm_kernel, grid=(1,),
        in_specs=[full(y.shape), full(xs.shape), full(z.shape), full(dexp.shape), full(nw.shape)],
        out_specs=full(y.shape), out_shape=jax.ShapeDtypeStruct(y.shape, F32),
        compiler_params=_params(1), name="ssd_decode_norm")(y, xs, z, dexp, nw)
    conv_new = jnp.concatenate([conv0[:, 1:], xbc[:, None, :]], axis=1)
    return ya, hn, conv_new


def _flash_next(st, h, s_t, v_t):
    m_s, l_s, acc_s = st
    m = m_s[h]
    m_new = jnp.maximum(m, jnp.max(s_t, axis=0, keepdims=True))
    alpha = jnp.exp(m - m_new)
    p = jnp.exp(s_t - m_new)
    m_s[h] = m_new
    l_s[h] = alpha * l_s[h] + jnp.sum(p, axis=0, keepdims=True)
    acc_s[h] = alpha * acc_s[h] + jnp.dot(v_t, p.astype(BF16), preferred_element_type=F32)


def _stage_kv(nb, h, width, k_ref, v_ref, kb, v_t):
    t = ATTN_TILE
    means = []
    for j in range(nb):
        kj = k_ref[j * t:(j + 1) * t, h * width:(h + 1) * width]
        kb[h, j * t:(j + 1) * t, :] = kj.astype(BF16)
        v_t[h, j] = v_ref[j * t:(j + 1) * t, h * width:(h + 1) * width].T.astype(BF16)
        means.append(jnp.mean(kj, axis=0, keepdims=True))
    return means


def _chunk_keys(kb, h, j0):
    t = ATTN_TILE
    return kb[h, pl.ds(pl.multiple_of(j0 * t, t), ATTN_CHUNK * t), :]


def _causal_chunk_mask(qi, j0, n_cols):
    t = ATTN_TILE
    ri = lax.broadcasted_iota(jnp.int32, (ATTN_CHUNK * t, n_cols), 0)
    ci = lax.broadcasted_iota(jnp.int32, (ATTN_CHUNK * t, n_cols), 1)
    for rep in range(1, n_cols // t):
        ci = jnp.where(ci >= rep * t, ci - t, ci)
    return ri + (j0 - qi) * t <= ci


def _chunked_flash(qi, st, s_buf, v_t, scores):
    m_s, l_s, acc_s = st
    ch = ATTN_CHUNK
    n_full = qi // ch
    j_last = n_full * ch
    for h in range(ATTN_HEADS):
        m_s[h] = jnp.full(m_s.shape[1:], NEG_INF, F32)
        l_s[h] = jnp.zeros(l_s.shape[1:], F32)
        acc_s[h] = jnp.zeros(acc_s.shape[1:], F32)
        s_buf[h, 0] = scores(h, j_last, True)

    def consume(slot, j0):
        for h in range(ATTN_HEADS):
            v = jnp.concatenate([v_t[h, j0 + b] for b in range(ch)], axis=1)
            _flash_next(st, h, s_buf[h, slot], v)

    def body(c, j_prev):
        slot = c % 2
        consume(slot, j_prev)
        for h in range(ATTN_HEADS):
            s_buf[h, 1 - slot] = scores(h, c * ch, False)
        return c * ch

    j_prev = lax.fori_loop(0, n_full, body, j_last)
    consume(n_full % 2, j_prev)


def _flash_scratch(nq, width, n_queries):
    t = ATTN_TILE
    return [pltpu.VMEM((ATTN_HEADS, nq * t, width), BF16), pltpu.VMEM((ATTN_HEADS, nq, width, t), BF16),
            pltpu.VMEM((ATTN_HEADS, 1, n_queries), F32), pltpu.VMEM((ATTN_HEADS, 1, n_queries), F32),
            pltpu.VMEM((ATTN_HEADS, width, n_queries), F32),
            pltpu.VMEM((ATTN_HEADS, 2, ATTN_CHUNK * t, n_queries), F32)]


def _moba_prompt_kernel(nb, q_ref, k_ref, v_ref, g_ref, o_ref, kb, v_t, m_s, l_s, acc_s, s_buf,
                        kmean, qs_s, bias_s):
    t = ATTN_TILE
    qi = pl.program_id(2)
    scale = HD_B ** -0.5
    st = (m_s, l_s, acc_s)

    @pl.when(qi == 0)
    def _():
        for h in range(ATTN_HEADS):
            kmean[h] = jnp.concatenate(_stage_kv(nb, h, HD_B, k_ref, v_ref, kb, v_t), axis=0)

    rowb = lax.broadcasted_iota(jnp.int32, (nb, t), 0)
    rowf = rowb.astype(F32)
    for h in range(ATTN_HEADS):
        q = q_ref[:, h * HD_B:(h + 1) * HD_B]
        qs = (q * scale).astype(BF16)
        gate = jnp.where(rowb < qi, _mm_nt_hp(kmean[h], q), NEG_INF)
        sel = jnp.zeros((nb, t), F32)
        for _ in range(min(MOBA_TOPK, nb)):
            mx = jnp.max(gate, axis=0, keepdims=True)
            first = jnp.min(jnp.where(gate == mx, rowf, float(nb)), axis=0, keepdims=True)
            hit = rowf == first
            sel = jnp.maximum(sel, jnp.where(hit, jnp.where(mx > NEG_INF, 1.0, 0.0), 0.0))
            gate = jnp.where(hit, NEG_INF, gate)
        bias_s[h] = jnp.where(rowb == qi, 0.0, jnp.where(sel > 0.0, 0.0, MASK_BIAS))[:, None, :]
        qs_s[h] = qs

    def scores(h, j0, last):
        s = _mm_nt(_chunk_keys(kb, h, j0), qs_s[h]).reshape(ATTN_CHUNK, t, t) + bias_s[h, pl.ds(j0, ATTN_CHUNK)]
        s = s.reshape(ATTN_CHUNK * t, t)
        return jnp.where(_causal_chunk_mask(qi, j0, t), s, NEG_INF) if last else s

    _chunked_flash(qi, st, s_buf, v_t, scores)
    for h in range(ATTN_HEADS):
        sl = slice(h * HD_B, (h + 1) * HD_B)
        o_ref[:, sl] = (acc_s[h] / l_s[h]).T * _silu(g_ref[:, sl])


def _moba_prompt(q, k, v, li, g, batch, length):
    t = ATTN_TILE
    nq = length // t
    assert H_B % ATTN_HEADS == 0 and nq % ATTN_CHUNK == 0
    w = ATTN_HEADS * HD_B
    blk = pl.BlockSpec((t, w), lambda b, h, i: (b * nq + i, h))
    seq = pl.BlockSpec((None, length, w), lambda b, h, i: (li, b, h))
    return pl.pallas_call(
        functools.partial(_moba_prompt_kernel, nq), grid=(batch, H_B // ATTN_HEADS, nq),
        in_specs=[blk, seq, seq, blk], out_specs=blk,
        out_shape=jax.ShapeDtypeStruct((batch * length, D_INNER_B), F32),
        scratch_shapes=_flash_scratch(nq, HD_B, t) + [pltpu.VMEM((ATTN_HEADS, nq, HD_B), F32),
                                                      pltpu.VMEM((ATTN_HEADS, t, HD_B), BF16),
                                                      pltpu.VMEM((ATTN_HEADS, nq, 1, t), F32)],
        compiler_params=_params(3), name="moba_prompt")(q, k, v, g)


def _lambda(lq1, lk1, lq2, lk2, lam_init):
    return (jnp.exp(jnp.sum(lq1[...] * lk1[...], axis=-1, keepdims=True))
            - jnp.exp(jnp.sum(lq2[...] * lk2[...], axis=-1, keepdims=True)) + lam_init)


def _diff_prompt_kernel(nb, lam_init, q_ref, k_ref, v_ref, g_ref, lq1, lk1, lq2, lk2, sw_ref, o_ref,
                        kb, v_t, m_s, l_s, acc_s, s_buf, q2_s):
    t = ATTN_TILE
    qi = pl.program_id(2)
    scale = DK_C ** -0.5
    st = (m_s, l_s, acc_s)

    @pl.when(qi == 0)
    def _():
        for h in range(ATTN_HEADS):
            _stage_kv(nb, h, DV_C, k_ref, v_ref, kb, v_t)

    lam = _lambda(lq1, lk1, lq2, lk2, lam_init)
    lane = lax.broadcasted_iota(jnp.int32, (t, DV_C), 1)
    for h in range(ATTN_HEADS):
        qs = q_ref[:, h * DV_C:(h + 1) * DV_C] * scale
        q2_s[h] = jnp.concatenate([jnp.where(lane < DK_C, qs, 0.0), jnp.where(lane >= DK_C, qs, 0.0)],
                                  axis=0).astype(BF16)

    def scores(h, j0, last):
        s = _mm_nt(_chunk_keys(kb, h, j0), q2_s[h])
        return jnp.where(_causal_chunk_mask(qi, j0, 2 * t), s, NEG_INF) if last else s

    _chunked_flash(qi, st, s_buf, v_t, scores)
    for h in range(ATTN_HEADS):
        sl = slice(h * DV_C, (h + 1) * DV_C)
        o_t = acc_s[h] / l_s[h]
        o = (o_t[:, :t] - lam * o_t[:, t:]).T
        o = o * lax.rsqrt(jnp.mean(o * o, axis=-1, keepdims=True) + EPS) * sw_ref[...] * (1.0 - lam_init)
        o_ref[:, sl] = o * _silu(g_ref[:, sl])


def _diff_prompt(q, k, v, li, g, lams, subln_w, lam_init, batch, length):
    t = ATTN_TILE
    nq = length // t
    assert H_C % ATTN_HEADS == 0 and nq % ATTN_CHUNK == 0
    w = ATTN_HEADS * DV_C
    blk = pl.BlockSpec((t, w), lambda b, h, i: (b * nq + i, h))
    seq = pl.BlockSpec((None, length, w), lambda b, h, i: (li, b, h))
    consts = [x.astype(F32).reshape(1, DK_C) for x in lams] + [subln_w.astype(F32).reshape(1, DV_C)]
    return pl.pallas_call(
        functools.partial(_diff_prompt_kernel, nq, lam_init), grid=(batch, H_C // ATTN_HEADS, nq),
        in_specs=[blk, seq, seq, blk] + [_const_spec(c.shape) for c in consts], out_specs=blk,
        out_shape=jax.ShapeDtypeStruct((batch * length, D_INNER_C), F32),
        scratch_shapes=_flash_scratch(nq, DV_C, 2 * t) + [pltpu.VMEM((ATTN_HEADS, 2 * t, DV_C), BF16)],
        compiler_params=_params(3), name="diff_prompt")(q, k, v, g, *consts)


def _lane_sum_matrix(seg):
    r = lax.broadcasted_iota(jnp.int32, (LANES, LANES), 0)
    c = lax.broadcasted_iota(jnp.int32, (LANES, LANES), 1)
    return jnp.where(r // seg == c // seg, 1.0, 0.0).astype(F32)


def _swap_halves(a):
    return pltpu.roll(a, LANES // 2, a.ndim - 1)


def _page_partial(qs, k_ref, v_ref, jmat, two_maps, want_ksum):
    kk = k_ref[...]
    vv = v_ref[...]
    rows = kk.shape[0] * kk.shape[1]
    s = _mm((kk * qs[None]).reshape(rows, LANES), jmat).reshape(kk.shape)
    m = jnp.max(s, axis=0)
    p = jnp.exp(s - m[None])
    l = jnp.sum(p, axis=0)
    accs = [jnp.sum(p * vv, axis=0)]
    if two_maps:
        p_sw = _swap_halves(p.reshape(rows, LANES)).reshape(kk.shape)
        accs.append(jnp.sum(p_sw * vv, axis=0))
    return m, l, accs, (jnp.sum(kk, axis=0) if want_ksum else None)


def _moba_decode_kernel(n_pages, pt_ref, q_ref, kn_ref, vn_ref, g_ref, *refs):
    k_refs = refs[:n_pages]
    v_refs = refs[n_pages:2 * n_pages]
    o_ref = refs[2 * n_pages]
    scale = HD_B ** -0.5
    ppb = MOBA_BLOCK // PAGE_SIZE
    nb = n_pages // ppb
    ones = _lane_sum_matrix(LANES)
    q = q_ref[0]
    qs = q * scale
    parts = [_page_partial(qs, k_refs[t], v_refs[t], ones, False, True) for t in range(n_pages)]
    kq = [sum(pt[3] for pt in parts[b * ppb:(b + 1) * ppb]) * (1.0 / MOBA_BLOCK) * q for b in range(nb)]
    gates = _mm_sel_rhs(jnp.concatenate(kq, axis=0), ones)
    gate = [gates[b * H_B:(b + 1) * H_B] for b in range(nb)]
    chosen = []
    for b in range(nb):
        rank = jnp.zeros((H_B, LANES), F32)
        for b2 in range(nb):
            if b2 != b:
                beats = (gate[b2] >= gate[b]) if b2 < b else (gate[b2] > gate[b])
                rank = rank + jnp.where(beats, 1.0, 0.0)
        chosen.append(rank < float(MOBA_TOPK))
    s_self = _mm_sel_rhs(qs * kn_ref[0], ones)
    mtot = s_self
    for t, (m, _, _, _) in enumerate(parts):
        mtot = jnp.maximum(mtot, jnp.where(chosen[t // ppb], m, NEG_INF))
    l_tot = jnp.exp(s_self - mtot)
    acc = l_tot * vn_ref[0]
    for t, (m, l, a, _) in enumerate(parts):
        w = jnp.where(chosen[t // ppb], jnp.exp(jnp.minimum(m - mtot, 0.0)), 0.0)
        l_tot = l_tot + w * l
        acc = acc + w * a[0]
    o_ref[0] = (acc / l_tot) * _silu(g_ref[0])


def _diff_decode_kernel(n_pages, lam_init, pt_ref, q_ref, kn_ref, vn_ref, g_ref,
                        lq1, lk1, lq2, lk2, sw_ref, *refs):
    k_refs = refs[:n_pages]
    v_refs = refs[n_pages:2 * n_pages]
    o_ref = refs[2 * n_pages]
    scale = DK_C ** -0.5
    lam = _lambda(lq1, lk1, lq2, lk2, lam_init)
    jmat = _lane_sum_matrix(DK_C)
    qs = q_ref[0] * scale
    parts = [_page_partial(qs, k_refs[t], v_refs[t], jmat, True, False) for t in range(n_pages)]
    s_self = _mm_sel_rhs(qs * kn_ref[0], jmat)
    mtot = s_self
    for m, _, _, _ in parts:
        mtot = jnp.maximum(mtot, m)
    p_self = jnp.exp(s_self - mtot)
    vn = vn_ref[0]
    l_tot = p_self
    acc_a = p_self * vn
    acc_b = _swap_halves(p_self) * vn
    for m, l, a, _ in parts:
        w = jnp.exp(m - mtot)
        l_tot = l_tot + w * l
        acc_a = acc_a + w * a[0]
        acc_b = acc_b + _swap_halves(w) * a[1]
    first = lax.broadcasted_iota(jnp.int32, (H_C, LANES), 1) < DK_C
    l_sw = _swap_halves(l_tot)
    o1 = jnp.where(first, acc_a, acc_b) / jnp.where(first, l_tot, l_sw)
    o2 = jnp.where(first, acc_b, acc_a) / jnp.where(first, l_sw, l_tot)
    o = o1 - lam * o2
    o = o * lax.rsqrt(jnp.mean(o * o, axis=-1, keepdims=True) + EPS) * sw_ref[...] * (1.0 - lam_init)
    o_ref[0] = o * _silu(g_ref[0])


def _decode_attn(kernel_fn, q, kn, vn, g, consts, cache_k, cache_v, li, pt_flat, n_pages, name):
    s = q.shape[0]
    heads, hd = cache_k.shape[3], cache_k.shape[4]
    r3 = lambda a: a.reshape(s, heads, hd)
    row = pl.BlockSpec((1, heads, hd), lambda i, pt: (i, 0, 0))

    def page(t):
        return pl.BlockSpec((None, None, PAGE_SIZE, heads, hd),
                            lambda i, pt: (li, pt[i * n_pages + t], 0, 0, 0))

    const_specs = [pl.BlockSpec(c.shape, lambda i, pt, nd=c.ndim: (0,) * nd) for c in consts]
    grid_spec = pltpu.PrefetchScalarGridSpec(
        num_scalar_prefetch=1, grid=(s,),
        in_specs=[row, row, row, row] + const_specs + [page(t) for t in range(n_pages)] * 2,
        out_specs=row)
    out = pl.pallas_call(
        kernel_fn, grid_spec=grid_spec, out_shape=jax.ShapeDtypeStruct((s, heads, hd), F32),
        compiler_params=_params(1), name=name)(
            pt_flat, r3(q), r3(kn), r3(vn), r3(g), *consts, *([cache_k] * n_pages), *([cache_v] * n_pages))
    return out.reshape(s, heads * hd)


def _rope_tables(pos, rot_dim, period):
    half = rot_dim // 2
    n = pos.shape[0]
    inv = ROPE_THETA ** (-jnp.arange(half, dtype=F32) * (2.0 / rot_dim))
    ang = pos.astype(F32)[:, None] * inv[None, :]
    cos, sin = jnp.cos(ang), jnp.sin(ang)
    c = jnp.concatenate([cos, cos, jnp.ones((n, period - rot_dim), F32)], axis=1)
    s1 = jnp.concatenate([-sin, jnp.zeros((n, period - half), F32)], axis=1)
    s2 = jnp.concatenate([jnp.zeros((n, half), F32), sin, jnp.zeros((n, period - rot_dim), F32)], axis=1)
    reps = LANES // period
    return tuple(jnp.tile(a, (1, reps)) for a in (c, s1, s2))


def kernel(x_prompt, x_sample, cache_moba_k, cache_moba_v, cache_diff_k, cache_diff_v, state_ssm, state_conv, page_table, p_prompt, p_sample, w_in_even, conv_w, conv_b, dt_bias, a_log, d_skip, ssm_norm_w, w_out_even, w_in_odd, lambda_q1, lambda_k1, lambda_q2, lambda_k2, subln_w, w_out_odd, ln_g, ln_b, w_ple_gate, w_ple_proj):
    bp, length, _ = x_prompt.shape
    s = x_sample.shape[0]
    assert x_sample.shape[1] == 1
    n_pages = page_table.shape[1]
    past_len = n_pages * PAGE_SIZE
    assert past_len % MOBA_BLOCK == 0 and length % ATTN_TILE == 0 and length % SSD_CHUNK == 0
    assert s % 8 == 0
    pt_flat = page_table.reshape(-1).astype(jnp.int32)
    tm_p = 256
    xp = x_prompt.reshape(bp * length, D_MODEL)
    xs = x_sample.reshape(s, D_MODEL)
    pos_p = jnp.arange(length, dtype=jnp.int32)
    pos_s = jnp.full((s,), past_len, jnp.int32)
    tabs = {("b", "p"): _rope_tables(pos_p, ROT_B, HD_B), ("b", "s"): _rope_tables(pos_s, ROT_B, HD_B),
            ("c", "p"): _rope_tables(pos_p, ROT_C, DK_C), ("c", "s"): _rope_tables(pos_s, ROT_C, DK_C)}

    n_even, n_odd = (DEPTH + 1) // 2, DEPTH // 2
    pm_kv, pd_kv, p_ssm, p_conv = None, None, [], []
    sm_k, sm_v, sd_k, sd_v, s_ssm, s_conv = [], [], [], [], [], []
    for i in range(DEPTH):
        li = i // 2
        post = functools.partial(_post, ln_g=ln_g[i], ln_b=ln_b[i], w_gate=w_ple_gate[i], w_proj=w_ple_proj[i])
        pp = p_prompt[i].reshape(bp * length, PLE_DIM)
        ps = p_sample[i].reshape(s, PLE_DIM)
        if i % 2 == 0:
            ssm_w = (conv_w[li], conv_b[li], a_log[li], d_skip[li], ssm_norm_w[li])
            w_outs = [w_out_even[li][:D_INNER_A], w_out_even[li][D_INNER_A:]]
            z, xbc, dt, dtT, q, k, v, g = _inproj_even(xp, tabs[("b", "p")], w_in_even[li], dt_bias[li], tm_p,
                                                       kv_slot=(n_even, li, pm_kv))
            pm_kv = (k, v)
            ya, st = _ssd_prompt(z, xbc, dt, dtT, *ssm_w, bp, length)
            yb = _moba_prompt(q, k, v, li, g, bp, length)
            xp = post([ya, yb], w_outs, xp, pp, tm=tm_p)
            p_ssm.append(st)
            p_conv.append(xbc.reshape(bp, length, CONV_DIM)[:, length - (CONV_W - 1):])
            z, xbc, dt, dtT, q, k, v, g = _inproj_even(xs, tabs[("b", "s")], w_in_even[li], dt_bias[li], s)
            ya, st, cv = _ssd_decode(z, xbc, dt, state_ssm, li, s_ssm, state_conv[li], *ssm_w)
            yb = _decode_attn(functools.partial(_moba_decode_kernel, n_pages), q, k, v, g, [],
                              cache_moba_k, cache_moba_v, li, pt_flat, n_pages, "moba_decode")
            xs = post([ya, yb], w_outs, xs, ps, tm=s)
            sm_k.append(k.reshape(s, 1, H_B, HD_B)); sm_v.append(v.reshape(s, 1, H_B, HD_B))
            s_ssm = [st]
            s_conv.append(cv)
        else:
            lam_init = 0.8 - 0.6 * math.exp(-0.3 * i)
            lams = (lambda_q1[li], lambda_k1[li], lambda_q2[li], lambda_k2[li])
            q, k, v, g = _inproj_odd(xp, tabs[("c", "p")], w_in_odd[li], tm_p, kv_slot=(n_odd, li, pd_kv))
            pd_kv = (k, v)
            y = _diff_prompt(q, k, v, li, g, lams, subln_w[li], lam_init, bp, length)
            xp = post([y], [w_out_odd[li]], xp, pp, tm=tm_p)
            q, k, v, g = _inproj_odd(xs, tabs[("c", "s")], w_in_odd[li], s)
            consts = [x.astype(F32).reshape(1, DK_C) for x in lams] + \
                     [subln_w[li].astype(F32).reshape(1, DV_C)]
            y = _decode_attn(functools.partial(_diff_decode_kernel, n_pages, lam_init), q, k, v, g, consts,
                             cache_diff_k, cache_diff_v, li, pt_flat, n_pages, "diff_decode")
            xs = post([y], [w_out_odd[li]], xs, ps, tm=s)
            sd_k.append(k.reshape(s, 1, H_C, 2 * DK_C)); sd_v.append(v.reshape(s, 1, H_C, DV_C))
    return (xp.reshape(bp, length, D_MODEL), xs.reshape(s, 1, D_MODEL),
            pm_kv[0].reshape(n_even, bp, length, H_B, HD_B), pm_kv[1].reshape(n_even, bp, length, H_B, HD_B),
            pd_kv[0].reshape(n_odd, bp, length, H_C, 2 * DK_C), pd_kv[1].reshape(n_odd, bp, length, H_C, DV_C),
            jnp.stack(p_ssm), jnp.stack(p_conv),
            jnp.stack(sm_k), jnp.stack(sm_v), jnp.stack(sd_k), jnp.stack(sd_v),
            s_ssm[0], jnp.stack(s_conv))
```
